```python
import math
import jax, jax.numpy as jnp
from jax import lax
import numpy as np

D_MODEL = 4096
BATCH = 2
SEQ = 4096
DEPTH = 2

CHUNK = 64
N_HEADS_A = 16
HEAD_DIM_A = 128
V_DIM_A = 2 * HEAD_DIM_A
WIDTH_A = N_HEADS_A * V_DIM_A
QK_COLS = N_HEADS_A * HEAD_DIM_A
QBLOCK = 128
GMLP_CHUNK = 128
N_GROUPS_B = 8
WIDTH_B = D_MODEL
GROUP_DIM_B = WIDTH_B // N_GROUPS_B
N_BRANCHES = 2
D_FF = 4 * D_MODEL
IN_COLS = 4 * QK_COLS + WIDTH_A + 2 * WIDTH_B + N_BRANCHES * D_MODEL
EPS = 1e-6

kernel_name = "hybrid_diffattn_gmlp_gated_block"


def rms_norm(x, g):
    xf = x.astype(jnp.float32)
    y = xf * lax.rsqrt(jnp.mean(xf * xf, axis=-1, keepdims=True) + EPS)
    return (y * g.astype(jnp.float32)).astype(x.dtype)


def alibi_slopes():
    h = jnp.arange(1, N_HEADS_A + 1, dtype=jnp.float32)
    return jnp.exp2(-8.0 * h / N_HEADS_A)


def lambda_init(layer_idx):
    return 0.8 - 0.6 * math.exp(-0.3 * (layer_idx - 1))


def diff_attention(q1, q2, k1, k2, v, lam):
    S = q1.shape[1]
    scale = HEAD_DIM_A ** -0.5
    slopes = alibi_slopes()
    pos = jnp.arange(S)
    outs = []
    for start in range(0, S, QBLOCK):
        end = start + QBLOCK
        qp = pos[start:end]
        kp = pos[:end]
        allowed = (kp[None, :] // CHUNK) <= (qp[:, None] // CHUNK)
        dist = jnp.abs(qp[:, None] - kp[None, :]).astype(jnp.float32)
        bias = jnp.where(allowed[None], -slopes[:, None, None] * dist, -jnp.inf)
        s1 = jnp.einsum('bqhd,bkhd->bhqk', q1[:, start:end], k1[:, :end]).astype(jnp.float32) * scale + bias
        s2 = jnp.einsum('bqhd,bkhd->bhqk', q2[:, start:end], k2[:, :end]).astype(jnp.float32) * scale + bias
        p = jax.nn.softmax(s1, axis=-1) - lam * jax.nn.softmax(s2, axis=-1)
        outs.append(jnp.einsum('bhqk,bkhe->bqhe', p.astype(v.dtype), v[:, :end]))
    return jnp.concatenate(outs, axis=1)


def spatial_gating(u, v, g_v, ws, b):
    B, S, _ = v.shape
    vn = rms_norm(v, g_v).reshape(B, S // GMLP_CHUNK, GMLP_CHUNK, N_GROUPS_B, GROUP_DIM_B)
    idx = jnp.arange(GMLP_CHUNK)
    mask = (idx[None, :] // CHUNK) <= (idx[:, None] // CHUNK)
    w = jnp.where(mask[None], ws, jnp.zeros_like(ws))
    mixed = jnp.einsum('gij,bnjgc->bnigc', w, vn) + b.T[None, None, :, :, None].astype(vn.dtype)
    return u * mixed.reshape(B, S, WIDTH_B)


def setup_inputs(seed: int = 0) -> dict:
    key = jax.random.key(seed)
    ks = jax.random.split(key, 20)
    f32 = jnp.float32
    L = DEPTH
    nrm = lambda k, shp, s: jax.random.normal(k, shp, f32) * s
    return {
        "x": jax.random.normal(ks[0], (BATCH, SEQ, D_MODEL), f32),
        "norm_mix_g": 1.0 + nrm(ks[1], (L, D_MODEL), 0.02),
        "w_in": nrm(ks[2], (L, D_MODEL, IN_COLS), D_MODEL ** -0.5),
        "qk_gain_q": 1.0 + nrm(ks[3], (L, HEAD_DIM_A), 0.02),
        "qk_gain_k": 1.0 + nrm(ks[4], (L, HEAD_DIM_A), 0.02),
        "lam_q1": nrm(ks[5], (L, HEAD_DIM_A), 0.1),
        "lam_k1": nrm(ks[6], (L, HEAD_DIM_A), 0.1),
        "lam_q2": nrm(ks[7], (L, HEAD_DIM_A), 0.1),
        "lam_k2": nrm(ks[8], (L, HEAD_DIM_A), 0.1),
        "diff_out_g": 1.0 + nrm(ks[9], (L, V_DIM_A), 0.02),
        "gmlp_v_g": 1.0 + nrm(ks[10], (L, WIDTH_B), 0.02),
        "gmlp_ws": nrm(ks[11], (L, N_GROUPS_B, GMLP_CHUNK, GMLP_CHUNK), GMLP_CHUNK ** -0.5),
        "gmlp_b": 1.0 + nrm(ks[12], (L, N_GROUPS_B, GMLP_CHUNK), 0.01),
        "w_o": nrm(ks[13], (L, D_MODEL, D_MODEL), D_MODEL ** -0.5),
        "norm_mlp_g": 1.0 + nrm(ks[14], (L, D_MODEL), 0.02),
        "w_up": nrm(ks[15], (L, D_MODEL, D_FF), D_MODEL ** -0.5),
        "w_down": nrm(ks[16], (L, D_FF, D_MODEL), D_FF ** -0.5),
    }


def reference(x, norm_mix_g, w_in, qk_gain_q, qk_gain_k, lam_q1, lam_k1, lam_q2, lam_k2,
              diff_out_g, gmlp_v_g, gmlp_ws, gmlp_b, w_o, norm_mlp_g, w_up, w_down):
    B, S, _ = x.shape
    splits = np.cumsum([QK_COLS, QK_COLS, QK_COLS, QK_COLS, WIDTH_A, WIDTH_B, WIDTH_B])
    for l in range(DEPTH):
        h = rms_norm(x, norm_mix_g[l])
        z = h @ w_in[l]
        q1, q2, k1, k2, va, ub, vb, gl = jnp.split(z, list(splits), axis=-1)
        hs = (B, S, N_HEADS_A, HEAD_DIM_A)
        q1 = rms_norm(q1.reshape(hs), qk_gain_q[l])
        q2 = rms_norm(q2.reshape(hs), qk_gain_q[l])
        k1 = rms_norm(k1.reshape(hs), qk_gain_k[l])
        k2 = rms_norm(k2.reshape(hs), qk_gain_k[l])
        va = va.reshape(B, S, N_HEADS_A, V_DIM_A)
        lam_i = lambda_init(l + 1)
        lam = (jnp.exp(jnp.sum(lam_q1[l].astype(jnp.float32) * lam_k1[l].astype(jnp.float32)))
               - jnp.exp(jnp.sum(lam_q2[l].astype(jnp.float32) * lam_k2[l].astype(jnp.float32)))
               + lam_i)
        oa = diff_attention(q1, q2, k1, k2, va, lam)
        y_a = (rms_norm(oa, diff_out_g[l]) * (1.0 - lam_i)).reshape(B, S, WIDTH_A)
        y_b = spatial_gating(jax.nn.gelu(ub), jax.nn.gelu(vb), gmlp_v_g[l], gmlp_ws[l], gmlp_b[l])
        gates = jax.nn.sigmoid(gl.astype(jnp.float32)).astype(x.dtype).reshape(B, S, N_BRANCHES, D_MODEL)
        merged = gates[:, :, 0] * y_a + gates[:, :, 1] * y_b
        x = x + merged @ w_o[l]
        h2 = rms_norm(x, norm_mlp_g[l])
        x = x + jnp.square(jax.nn.relu(h2 @ w_up[l])) @ w_down[l]
    return x
```

```python
import functools
import math

import jax
import jax.numpy as jnp
from jax import lax
from jax.experimental import pallas as pl
from jax.experimental.pallas import tpu as pltpu

CHUNK = 64
N_HEADS_A = 16
HEAD_DIM_A = 128
V_DIM_A = 2 * HEAD_DIM_A
GMLP_CHUNK = 128
N_GROUPS_B = 8
EPS = 1e-6
MASK_VALUE = -1e30

V7X_VMEM_BYTES = 64 * 1024 * 1024
VMEM_LIMIT_BYTES = V7X_VMEM_BYTES - 8 * 1024 * 1024

MM_BM = 1024
MM_BN = 1024
MM_BK_FULL = 4096
MM_BK_SPLIT = 2048
NORM_ROWS = 256
ATT_BQ = 512
ATT_BK = 512
GATE_ROWS = 256


def _params(semantics):
    return pltpu.CompilerParams(dimension_semantics=semantics, vmem_limit_bytes=VMEM_LIMIT_BYTES)


def _rmsnorm_kernel(x_ref, g_ref, o_ref):
    x = x_ref[...]
    ms = jnp.mean(x * x, axis=-1, keepdims=True)
    o_ref[...] = (x * lax.rsqrt(ms + EPS) * g_ref[...]).astype(o_ref.dtype)


def _rmsnorm(x, g):
    m, d = x.shape
    return pl.pallas_call(
        _rmsnorm_kernel,
        grid=(m // NORM_ROWS,),
        in_specs=[pl.BlockSpec((NORM_ROWS, d), lambda i: (i, 0)),
                  pl.BlockSpec((1, d), lambda i: (0, 0))],
        out_specs=pl.BlockSpec((NORM_ROWS, d), lambda i: (i, 0)),
        out_shape=jax.ShapeDtypeStruct((m, d), jnp.bfloat16),
        compiler_params=_params(("parallel",)),
        name="rmsnorm",
    )(x, g.reshape(1, d))


def _matmul_kernel(*refs, epilogue, nk):
    if epilogue == "residual":
        x_ref, w_ref, r_ref, o_ref = refs[:4]
        rest = refs[4:]
    else:
        x_ref, w_ref, o_ref = refs[:3]
        r_ref = None
        rest = refs[3:]

    def finish(acc):
        if epilogue == "relu2":
            acc = jnp.square(jnp.maximum(acc, 0.0))
        elif epilogue == "residual":
            acc = r_ref[...] + acc
        o_ref[...] = acc.astype(o_ref.dtype)

    part = jnp.dot(x_ref[...], w_ref[...], preferred_element_type=jnp.float32)
    if nk == 1:
        finish(part)
        return

    acc_ref, = rest
    k = pl.program_id(2)

    @pl.when(k == 0)
    def _():
        acc_ref[...] = part

    @pl.when(jnp.logical_and(k > 0, k < nk - 1))
    def _():
        acc_ref[...] += part

    @pl.when(k == nk - 1)
    def _():
        finish(acc_ref[...] + part)


def _matmul(x, w, *, epilogue="none", residual=None, out_dtype=jnp.bfloat16):
    m, kdim = x.shape
    _, n = w.shape
    bm, bn = min(MM_BM, m), min(MM_BN, n)
    bk = kdim if kdim <= MM_BK_FULL else MM_BK_SPLIT
    nk = kdim // bk
    in_specs = [pl.BlockSpec((bm, bk), lambda i, j, k: (i, k)),
                pl.BlockSpec((bk, bn), lambda i, j, k: (k, j))]
    args = [x, w]
    if epilogue == "residual":
        in_specs.append(pl.BlockSpec((bm, bn), lambda i, j, k: (i, j)))
        args.append(residual)
    scratch = [pltpu.VMEM((bm, bn), jnp.float32)] if nk > 1 else []
    return pl.pallas_call(
        functools.partial(_matmul_kernel, epilogue=epilogue, nk=nk),
        grid=(m // bm, n // bn, nk),
        in_specs=in_specs,
        out_specs=pl.BlockSpec((bm, bn), lambda i, j, k: (i, j)),
        out_shape=jax.ShapeDtypeStruct((m, n), out_dtype),
        scratch_shapes=scratch,
        compiler_params=_params(("parallel", "parallel", "arbitrary")),
        name="matmul_" + epilogue,
    )(*args)


def _attention_kernel(q1_ref, q2_ref, k1_ref, k2_ref, v_ref, gq_ref, gk_ref,
                      lq1_ref, lk1_ref, lq2_ref, lk2_ref, dog_ref, o_ref,
                      kn1_ref, kn2_ref, acc1_ref, acc2_ref, *, seq, lam_init):
    h = pl.program_id(1)
    qi = pl.program_id(2)
    bq, bk = ATT_BQ, ATT_BK
    f32 = jnp.float32

    def qk_norm(x, g):
        x = x.astype(f32)
        ms = jnp.mean(x * x, axis=-1, keepdims=True)
        return x * lax.rsqrt(ms + EPS) * g

    @pl.when(qi == 0)
    def _():
        def body(i, _):
            rows = pl.ds(pl.multiple_of(i * bk, bk), bk)
            kn1_ref[rows, :] = qk_norm(k1_ref[0, rows, :], gk_ref[...]).astype(kn1_ref.dtype)
            kn2_ref[rows, :] = qk_norm(k2_ref[0, rows, :], gk_ref[...]).astype(kn2_ref.dtype)
            return 0
        lax.fori_loop(0, seq // bk, body, 0)

    scale = HEAD_DIM_A ** -0.5
    q1 = (qk_norm(q1_ref[0], gq_ref[...]) * scale).astype(jnp.bfloat16)
    q2 = (qk_norm(q2_ref[0], gq_ref[...]) * scale).astype(jnp.bfloat16)

    slope = jnp.exp2(jnp.full((1, 1), -8.0 / N_HEADS_A, f32) * (h + 1).astype(f32))
    col = lax.broadcasted_iota(jnp.int32, (1, bk), 1).astype(f32)
    key_bias = slope * col

    def scores(q, k):
        return lax.dot_general(q, k, (((1,), (1,)), ((), ())), preferred_element_type=f32)

    def update(s, m, l, acc_ref, v):
        m_new = jnp.maximum(m, jnp.max(s, axis=-1, keepdims=True))
        alpha = jnp.exp(m - m_new)
        p = jnp.exp(s - m_new)
        l_new = alpha * l + jnp.sum(p, axis=-1, keepdims=True)
        acc_ref[...] = alpha * acc_ref[...] + jnp.dot(p.astype(v.dtype), v,
                                                      preferred_element_type=f32)
        return m_new, l_new

    acc1_ref[...] = jnp.zeros_like(acc1_ref)
    acc2_ref[...] = jnp.zeros_like(acc2_ref)
    m0 = jnp.full((bq, 1), MASK_VALUE, f32)
    l0 = jnp.zeros((bq, 1), f32)

    def past_block(kj, carry):
        m1, l1, m2, l2 = carry
        rows = pl.ds(pl.multiple_of(kj * bk, bk), bk)
        v = v_ref[0, rows, :]
        bias = key_bias + slope * ((kj - qi) * bk).astype(f32)
        m1, l1 = update(scores(q1, kn1_ref[rows, :]) + bias, m1, l1, acc1_ref, v)
        m2, l2 = update(scores(q2, kn2_ref[rows, :]) + bias, m2, l2, acc2_ref, v)
        return m1, l1, m2, l2

    m1, l1, m2, l2 = lax.fori_loop(0, qi, past_block, (m0, l0, m0, l0))

    rows = pl.ds(pl.multiple_of(qi * bk, bk), bk)
    r = lax.broadcasted_iota(jnp.int32, (bq, bk), 0)
    c = lax.broadcasted_iota(jnp.int32, (bq, bk), 1)
    dist = jnp.abs(r - c).astype(f32)
    diag_bias = jnp.where((c // CHUNK) <= (r // CHUNK), slope * (r.astype(f32) - dist), MASK_VALUE)
    v = v_ref[0, rows, :]
    m1, l1 = update(scores(q1, kn1_ref[rows, :]) + diag_bias, m1, l1, acc1_ref, v)
    m2, l2 = update(scores(q2, kn2_ref[rows, :]) + diag_bias, m2, l2, acc2_ref, v)

    lam = (jnp.exp(jnp.sum(lq1_ref[...] * lk1_ref[...], keepdims=True))
           - jnp.exp(jnp.sum(lq2_ref[...] * lk2_ref[...], keepdims=True)) + lam_init)
    o = acc1_ref[...] / l1 - lam * (acc2_ref[...] / l2)
    ms = jnp.mean(o * o, axis=-1, keepdims=True)
    y = o * lax.rsqrt(ms + EPS) * dog_ref[...] * (1.0 - lam_init)
    o_ref[0] = y.astype(o_ref.dtype)


def _attention(z3, gq, gk, lq1, lk1, lq2, lk2, dog, lam_init):
    b, s, _ = z3.shape
    bq = ATT_BQ
    qk_cols = N_HEADS_A * HEAD_DIM_A
    k1_blk = 2 * qk_cols // HEAD_DIM_A
    k2_blk = 3 * qk_cols // HEAD_DIM_A
    v_blk = 4 * qk_cols // V_DIM_A
    q_spec = lambda off: pl.BlockSpec((1, bq, HEAD_DIM_A), lambda bi, hi, qi: (bi, qi, off + hi))
    kv_spec = lambda w, off: pl.BlockSpec((1, s, w), lambda bi, hi, qi: (bi, 0, off + hi))
    vec_spec = lambda w: pl.BlockSpec((1, w), lambda bi, hi, qi: (0, 0))
    row = lambda a: a.reshape(1, -1)
    return pl.pallas_call(
        functools.partial(_attention_kernel, seq=s, lam_init=lam_init),
        grid=(b, N_HEADS_A, s // bq),
        in_specs=[q_spec(0), q_spec(qk_cols // HEAD_DIM_A),
                  kv_spec(HEAD_DIM_A, k1_blk), kv_spec(HEAD_DIM_A, k2_blk), kv_spec(V_DIM_A, v_blk),
                  vec_spec(HEAD_DIM_A), vec_spec(HEAD_DIM_A),
                  vec_spec(HEAD_DIM_A), vec_spec(HEAD_DIM_A), vec_spec(HEAD_DIM_A), vec_spec(HEAD_DIM_A),
                  vec_spec(V_DIM_A)],
        out_specs=pl.BlockSpec((1, bq, V_DIM_A), lambda bi, hi, qi: (bi, qi, hi)),
        out_shape=jax.ShapeDtypeStruct((b, s, N_HEADS_A * V_DIM_A), jnp.bfloat16),
        scratch_shapes=[pltpu.VMEM((s, HEAD_DIM_A), jnp.bfloat16),
                        pltpu.VMEM((s, HEAD_DIM_A), jnp.bfloat16),
                        pltpu.VMEM((bq, V_DIM_A), jnp.float32),
                        pltpu.VMEM((bq, V_DIM_A), jnp.float32)],
        compiler_params=_params(("parallel", "parallel", "arbitrary")),
        name="diff_attention",
    )(z3, z3, z3, z3, z3, row(gq), row(gk), row(lq1), row(lk1), row(lq2), row(lk2), row(dog))


def _gate_kernel(u_ref, v_ref, g0_ref, g1_ref, ya_ref, gv_ref, ws_ref, bt_ref, o_ref, *, width):
    f32 = jnp.float32
    gd = width // N_GROUPS_B
    v = jax.nn.gelu(v_ref[...].astype(f32))
    ms = jnp.mean(v * v, axis=-1, keepdims=True)
    vn = (v * lax.rsqrt(ms + EPS) * gv_ref[...]).astype(jnp.bfloat16)
    i = lax.broadcasted_iota(jnp.int32, (GMLP_CHUNK, GMLP_CHUNK), 0)
    j = lax.broadcasted_iota(jnp.int32, (GMLP_CHUNK, GMLP_CHUNK), 1)
    allowed = (j // CHUNK) <= (i // CHUNK)
    for g in range(N_GROUPS_B):
        wm = jnp.where(allowed, ws_ref[g], 0.0).astype(jnp.bfloat16)
        bias = bt_ref[:, g:g + 1]
        cols = slice(g * gd, (g + 1) * gd)
        for wdw in range(GATE_ROWS // GMLP_CHUNK):
            rows = slice(wdw * GMLP_CHUNK, (wdw + 1) * GMLP_CHUNK)
            mixed = jnp.dot(wm, vn[rows, cols], preferred_element_type=f32) + bias
            y_b = jax.nn.gelu(u_ref[rows, cols].astype(f32)) * mixed
            merged = (jax.nn.sigmoid(g0_ref[rows, cols].astype(f32)) * ya_ref[rows, cols].astype(f32)
                      + jax.nn.sigmoid(g1_ref[rows, cols].astype(f32)) * y_b)
            o_ref[rows, cols] = merged.astype(o_ref.dtype)


def _gate_merge(z, ya, gv, ws, b, width, u_off):
    m = z.shape[0]
    zspec = lambda blk: pl.BlockSpec((GATE_ROWS, width), lambda i: (i, blk))
    ublk = u_off // width
    return pl.pallas_call(
        functools.partial(_gate_kernel, width=width),
        grid=(m // GATE_ROWS,),
        in_specs=[zspec(ublk), zspec(ublk + 1), zspec(ublk + 2), zspec(ublk + 3),
                  pl.BlockSpec((GATE_ROWS, width), lambda i: (i, 0)),
                  pl.BlockSpec((1, width), lambda i: (0, 0)),
                  pl.BlockSpec((N_GROUPS_B, GMLP_CHUNK, GMLP_CHUNK), lambda i: (0, 0, 0)),
                  pl.BlockSpec((GMLP_CHUNK, N_GROUPS_B), lambda i: (0, 0))],
        out_specs=pl.BlockSpec((GATE_ROWS, width), lambda i: (i, 0)),
        out_shape=jax.ShapeDtypeStruct((m, width), jnp.bfloat16),
        compiler_params=_params(("parallel",)),
        name="gmlp_gate_merge",
    )(z, z, z, z, ya, gv.reshape(1, width), ws, b.T)


def _lambda_init(layer_idx):
    return 0.8 - 0.6 * math.exp(-0.3 * (layer_idx - 1))


def kernel(x, norm_mix_g, w_in, qk_gain_q, qk_gain_k, lam_q1, lam_k1, lam_q2, lam_k2, diff_out_g,
           gmlp_v_g, gmlp_ws, gmlp_b, w_o, norm_mlp_g, w_up, w_down):
    b, s, d = x.shape
    depth = w_in.shape[0]
    bf16 = jnp.bfloat16
    qk_cols = N_HEADS_A * HEAD_DIM_A
    u_off = 4 * qk_cols + N_HEADS_A * V_DIM_A
    xf = x.reshape(b * s, d)
    for l in range(depth):
        h = _rmsnorm(xf, norm_mix_g[l])
        z = _matmul(h, w_in[l].astype(bf16))
        ya = _attention(z.reshape(b, s, -1), qk_gain_q[l], qk_gain_k[l], lam_q1[l], lam_k1[l],
                        lam_q2[l], lam_k2[l], diff_out_g[l], _lambda_init(l + 1))
        merged = _gate_merge(z, ya.reshape(b * s, -1), gmlp_v_g[l], gmlp_ws[l], gmlp_b[l], d, u_off)
        xf = _matmul(merged, w_o[l].astype(bf16), epilogue="residual", residual=xf,
                     out_dtype=jnp.float32)
        h2 = _rmsnorm(xf, norm_mlp_g[l])
        a = _matmul(h2, w_up[l].astype(bf16), epilogue="relu2")
        xf = _matmul(a, w_down[l].astype(bf16), epilogue="residual", residual=xf,
                     out_dtype=jnp.float32)
    return xf.reshape(b, s, d)
```

```python
import functools
import math

import jax
import jax.numpy as jnp
from jax import lax
from jax.experimental import pallas as pl
from jax.experimental.pallas import tpu as pltpu

CHUNK = 64
N_HEADS_A = 16
HEAD_DIM_A = 128
V_DIM_A = 2 * HEAD_DIM_A
GMLP_CHUNK = 128
N_GROUPS_B = 8
EPS = 1e-6
MASK_VALUE = -1e30
LOG2E = math.log2(math.e)

V7X_VMEM_BYTES = 64 * 1024 * 1024
VMEM_LIMIT_BYTES = V7X_VMEM_BYTES - 8 * 1024 * 1024
LANES = 128

MM_BM = 1024
MM_BN = 1024
MM_BK = 4096
NORM_ROWS = 256
ATT_BQ = 1024
ATT_BK = 512
ATT_STRIP = 64
GATE_ROWS = 256


def _params(semantics):
    return pltpu.CompilerParams(dimension_semantics=semantics, vmem_limit_bytes=VMEM_LIMIT_BYTES)


def _rmsnorm_kernel(x_ref, g_ref, o_ref):
    x = x_ref[...]
    ms = jnp.mean(x * x, axis=-1, keepdims=True)
    o_ref[...] = (x * lax.rsqrt(ms + EPS) * g_ref[...]).astype(o_ref.dtype)


def _rmsnorm(x, g):
    m, d = x.shape
    return pl.pallas_call(
        _rmsnorm_kernel,
        grid=(m // NORM_ROWS,),
        in_specs=[pl.BlockSpec((NORM_ROWS, d), lambda i: (i, 0)),
                  pl.BlockSpec((1, d), lambda i: (0, 0))],
        out_specs=pl.BlockSpec((NORM_ROWS, d), lambda i: (i, 0)),
        out_shape=jax.ShapeDtypeStruct((m, d), jnp.bfloat16),
        compiler_params=_params(("parallel",)),
        name="rmsnorm",
    )(x, g.reshape(1, d))


def _wcast_matmul_kernel(*refs, epilogue, has_side, chunk, head_width):
    refs = list(refs)
    x_ref, w_ref = refs[0], refs[1]
    pos = 2
    r_ref = g_ref = side_in = side_out = None
    if epilogue == "residual":
        r_ref = refs[pos]; pos += 1
    if epilogue == "headnorm":
        g_ref = refs[pos]; pos += 1
    if has_side:
        side_in = refs[pos]; pos += 1
    o_ref = refs[pos]; pos += 1
    if has_side:
        side_out = refs[pos]; pos += 1
    panel0, panel1 = refs[pos], refs[pos + 1]

    n = pl.program_id(0)
    m = pl.program_id(1)
    rows = pl.ds(pl.multiple_of(m * chunk, chunk), chunk)

    def cast_chunk(panel):
        panel[rows, :] = w_ref[...].astype(panel.dtype)

    def compute(panel):
        acc = jnp.dot(x_ref[...], panel[...], preferred_element_type=jnp.float32)
        if epilogue in ("headnorm", "heads"):
            for t in range(acc.shape[1] // head_width):
                cols = slice(t * head_width, (t + 1) * head_width)
                xh = acc[:, cols]
                if epilogue == "headnorm":
                    ms = jnp.mean(xh * xh, axis=-1, keepdims=True)
                    xh = xh * lax.rsqrt(ms + EPS) * g_ref[:, cols]
                o_ref[t] = xh.astype(o_ref.dtype)
        else:
            if epilogue == "relu2":
                acc = jnp.square(jnp.maximum(acc, 0.0))
            elif epilogue == "residual":
                acc = r_ref[...] + acc
            o_ref[...] = acc.astype(o_ref.dtype)
        if has_side:
            side_out[...] = side_in[...].astype(side_out.dtype)

    @pl.when(n == 0)
    def _():
        cast_chunk(panel0)

    @pl.when(jnp.logical_and(n > 0, n % 2 == 1))
    def _():
        cast_chunk(panel1)
        compute(panel0)

    @pl.when(jnp.logical_and(n > 0, n % 2 == 0))
    def _():
        cast_chunk(panel0)
        compute(panel1)


def _wcast_matmul(x, w, layer, *, col_off=0, ncols=None, bn=MM_BN, epilogue="none", residual=None,
                  colgain=None, head_width=None, out_dtype=jnp.bfloat16, side=None):
    m, kdim = x.shape
    n = w.shape[2] - col_off if ncols is None else ncols
    bm = min(MM_BM, m)
    nm, nn = m // bm, n // bn
    chunk = kdim // nm
    col0 = col_off // bn
    row_blk = lambda ni, mi: jnp.where(ni > 0, mi, 0)
    col_blk = lambda ni: jnp.maximum(ni - 1, 0)
    in_specs = [pl.BlockSpec((bm, kdim), lambda ni, mi: (row_blk(ni, mi), 0)),
                pl.BlockSpec((None, chunk, bn),
                             lambda ni, mi: (layer, jnp.where(ni < nn, mi, nm - 1),
                                             col0 + jnp.minimum(ni, nn - 1)))]
    args = [x, w]
    if epilogue == "residual":
        in_specs.append(pl.BlockSpec((bm, bn), lambda ni, mi: (row_blk(ni, mi), col_blk(ni))))
        args.append(residual)
    if epilogue == "headnorm":
        in_specs.append(pl.BlockSpec((1, bn), lambda ni, mi: (0, col_blk(ni))))
        args.append(colgain.reshape(1, n))
    if head_width is None:
        out_specs = [pl.BlockSpec((bm, bn), lambda ni, mi: (row_blk(ni, mi), col_blk(ni)))]
        out_shape = [jax.ShapeDtypeStruct((m, n), out_dtype)]
    else:
        out_specs = [pl.BlockSpec((bn // head_width, bm, head_width),
                                  lambda ni, mi: (col_blk(ni), row_blk(ni, mi), 0))]
        out_shape = [jax.ShapeDtypeStruct((n // head_width, m, head_width), out_dtype)]
    if side is not None:
        _, srows, scols = side.shape
        sblk = srows // (nn * nm)
        side_idx = lambda ni, mi: (col_blk(ni) * nm + row_blk(ni, mi), 0)
        in_specs.append(pl.BlockSpec((None, sblk, scols), lambda ni, mi: (layer,) + side_idx(ni, mi)))
        args.append(side)
        out_specs.append(pl.BlockSpec((sblk, scols), side_idx))
        out_shape.append(jax.ShapeDtypeStruct((srows, scols), jnp.bfloat16))
    outs = pl.pallas_call(
        functools.partial(_wcast_matmul_kernel, epilogue=epilogue, has_side=side is not None,
                          chunk=chunk, head_width=head_width),
        grid=(nn + 1, nm),
        in_specs=in_specs,
        out_specs=out_specs,
        out_shape=out_shape,
        scratch_shapes=[pltpu.VMEM((kdim, bn), jnp.bfloat16), pltpu.VMEM((kdim, bn), jnp.bfloat16)],
        compiler_params=_params(("arbitrary", "arbitrary")),
        name="wcast_matmul_" + epilogue,
    )(*args)
    return outs if side is not None else outs[0]


def _ksplit_matmul_kernel(x_ref, w_ref, r_ref, o_ref):
    k = pl.program_id(2)

    @pl.when(k == 0)
    def _():
        o_ref[...] = r_ref[...] + jnp.dot(x_ref[...], w_ref[...], preferred_element_type=jnp.float32)

    @pl.when(k > 0)
    def _():
        o_ref[...] += jnp.dot(x_ref[...], w_ref[...], preferred_element_type=jnp.float32)


def _ksplit_matmul(x, w, residual):
    m, kdim = x.shape
    _, n = w.shape
    bm, bn, bk = min(MM_BM, m), min(MM_BN, n), min(MM_BK, kdim)
    return pl.pallas_call(
        _ksplit_matmul_kernel,
        grid=(m // bm, n // bn, kdim // bk),
        in_specs=[pl.BlockSpec((bm, bk), lambda i, j, k: (i, k)),
                  pl.BlockSpec((bk, bn), lambda i, j, k: (k, j)),
                  pl.BlockSpec((bm, bn), lambda i, j, k: (i, j))],
        out_specs=pl.BlockSpec((bm, bn), lambda i, j, k: (i, j)),
        out_shape=jax.ShapeDtypeStruct((m, n), jnp.float32),
        compiler_params=_params(("parallel", "parallel", "arbitrary")),
        name="ksplit_matmul_residual",
    )(x, w, residual)


def _attention_kernel(q1_ref, q2_ref, k1_ref, k2_ref, v_ref,
                      lq1_ref, lk1_ref, lq2_ref, lk2_ref, dog_ref, o_ref,
                      kn1_ref, kn2_ref, dbias_ref, s1_ref, s2_ref, p1_ref, p2_ref,
                      acc1_ref, acc2_ref, m1_ref, l1_ref, a1_ref, m2_ref, l2_ref, q1a_ref, q2a_ref,
                      *, seq, lam_init):
    h = pl.program_id(1)
    bq, bk = ATT_BQ, ATT_BK
    f32, bf16 = jnp.float32, jnp.bfloat16

    slope = jnp.exp2(jnp.full((1, 1), -8.0 / N_HEADS_A, f32) * (h + 1).astype(f32)) * LOG2E
    lane = lax.broadcasted_iota(jnp.int32, (bk, LANES), 1)

    def key_chunk(i, _):
        rows = pl.ds(pl.multiple_of(i * bk, bk), bk)
        pos = (lax.broadcasted_iota(jnp.int32, (bk, 1), 0) + i * bk).astype(f32)
        bias = slope * pos
        hi = bias.astype(bf16).astype(f32)
        mid = (bias - hi).astype(bf16).astype(f32)
        lo = (bias - hi - mid).astype(bf16).astype(f32)
        extra = jnp.where(lane == 0, hi, jnp.where(lane == 1, mid, jnp.where(lane == 2, lo, 0.0)))
        extra = extra.astype(bf16)
        kn1_ref[rows, :HEAD_DIM_A] = k1_ref[0, rows, :]
        kn2_ref[rows, :HEAD_DIM_A] = k2_ref[0, rows, :]
        kn1_ref[rows, HEAD_DIM_A:] = extra
        kn2_ref[rows, HEAD_DIM_A:] = extra
        return 0

    lax.fori_loop(0, seq // bk, key_chunk, 0)
    qlane = lax.broadcasted_iota(jnp.int32, (bq, LANES), 1)
    ones = jnp.where(qlane < 3, 1.0, 0.0).astype(bf16)
    q1a_ref[:, HEAD_DIM_A:] = ones
    q2a_ref[:, HEAD_DIM_A:] = ones
    n_diag = bq // bk
    for d in range(n_diag):
        r = lax.broadcasted_iota(jnp.int32, (bq, bk), 0)
        c = lax.broadcasted_iota(jnp.int32, (bq, bk), 1) + d * bk
        future = jnp.maximum(c - r, 0).astype(f32)
        dbias_ref[d] = jnp.where((c // CHUNK) <= (r // CHUNK), -2.0 * slope * future, MASK_VALUE)

    lam = (jnp.exp(jnp.sum(lq1_ref[...] * lk1_ref[...], keepdims=True))
           - jnp.exp(jnp.sum(lq2_ref[...] * lk2_ref[...], keepdims=True)) + lam_init)

    def scores(q, k):
        return lax.dot_general(q, k, (((1,), (1,)), ((), ())), preferred_element_type=f32)

    def softmax(s_ref, p_ref, m_ref, l_ref, diag):
        m_prev = m_ref[...]
        s_all = s_ref[...] if diag is None else s_ref[...] + dbias_ref[diag]
        m_next = jnp.maximum(m_prev, jnp.max(s_all, axis=1, keepdims=True))
        alpha = jnp.exp2(m_prev - m_next)
        m_ref[...] = m_next
        for t in range(bq // ATT_STRIP):
            rows = slice(t * ATT_STRIP, (t + 1) * ATT_STRIP)
            s = s_ref[rows, :] if diag is None else s_ref[rows, :] + dbias_ref[diag, rows, :]
            p = jnp.exp2(s - jnp.tile(m_next[rows, :], (1, bk // LANES)))
            psum = p[:, :LANES]
            for j in range(1, bk // LANES):
                psum = psum + p[:, j * LANES:(j + 1) * LANES]
            l_ref[rows, :] = alpha[rows, :] * l_ref[rows, :] + psum
            p_ref[rows, :] = p.astype(p_ref.dtype)
        return alpha

    def accumulate(acc_ref, alpha, p_ref, v):
        acc_ref[...] = (jnp.tile(alpha, (1, V_DIM_A // LANES)) * acc_ref[...]
                        + jnp.dot(p_ref[...], v, preferred_element_type=f32))

    def key_rows(kj):
        return pl.ds(pl.multiple_of(kj * bk, bk), bk)

    def query_block(qi, _):
        qrows = pl.ds(pl.multiple_of(qi * bq, bq), bq)
        q1a_ref[:, :HEAD_DIM_A] = q1_ref[0, qrows, :]
        q2a_ref[:, :HEAD_DIM_A] = q2_ref[0, qrows, :]

        s2_ref[...] = scores(q2a_ref[...], kn2_ref[key_rows(0), :])
        for ref in (acc1_ref, acc2_ref, l1_ref, l2_ref, p1_ref):
            ref[...] = jnp.zeros_like(ref)
        m1_ref[...] = jnp.full_like(m1_ref, MASK_VALUE)
        m2_ref[...] = jnp.full_like(m2_ref, MASK_VALUE)
        a1_ref[...] = jnp.ones_like(a1_ref)

        def stage(kb, diag, last):
            accumulate(acc1_ref, a1_ref[...], p1_ref, v_ref[0, key_rows(jnp.maximum(kb - 1, 0)), :])
            alpha2 = softmax(s2_ref, p2_ref, m2_ref, l2_ref, diag)
            s1_ref[...] = scores(q1a_ref[...], kn1_ref[key_rows(kb), :])
            a1_ref[...] = softmax(s1_ref, p1_ref, m1_ref, l1_ref, diag)
            accumulate(acc2_ref, alpha2, p2_ref, v_ref[0, key_rows(kb), :])
            if not last:
                s2_ref[...] = scores(q2a_ref[...], kn2_ref[key_rows(kb + 1), :])

        def past_block(kj, _):
            stage(kj, None, False)
            return 0

        first_diag = qi * n_diag
        lax.fori_loop(0, first_diag, past_block, 0)
        for d in range(n_diag):
            stage(first_diag + d, d, d == n_diag - 1)
        accumulate(acc1_ref, a1_ref[...], p1_ref, v_ref[0, key_rows(first_diag + n_diag - 1), :])

        w1 = 1.0 / jnp.sum(l1_ref[...], axis=1, keepdims=True)
        w2 = lam / jnp.sum(l2_ref[...], axis=1, keepdims=True)
        o = acc1_ref[...] * w1 - acc2_ref[...] * w2
        ms = jnp.mean(o * o, axis=-1, keepdims=True)
        y = o * lax.rsqrt(ms + EPS) * dog_ref[...] * (1.0 - lam_init)
        o_ref[0, qrows, :] = y.astype(o_ref.dtype)
        return 0

    lax.fori_loop(0, seq // bq, query_block, 0)


def _attention(qkh, vh, b, lq1, lk1, lq2, lk2, dog, lam_init):
    s = qkh.shape[1] // b
    bq, bk = ATT_BQ, ATT_BK
    head_spec = lambda w, off: pl.BlockSpec((1, s, w), lambda bi, hi: (off + hi, bi, 0))
    vec_spec = lambda w: pl.BlockSpec((1, w), lambda bi, hi: (0, 0))
    row = lambda a: a.reshape(1, -1)
    f32, bf16 = jnp.float32, jnp.bfloat16
    stat = pltpu.VMEM((bq, LANES), f32)
    return pl.pallas_call(
        functools.partial(_attention_kernel, seq=s, lam_init=lam_init),
        grid=(b, N_HEADS_A),
        in_specs=[head_spec(HEAD_DIM_A, 0), head_spec(HEAD_DIM_A, N_HEADS_A),
                  head_spec(HEAD_DIM_A, 2 * N_HEADS_A), head_spec(HEAD_DIM_A, 3 * N_HEADS_A),
                  head_spec(V_DIM_A, 0),
                  vec_spec(HEAD_DIM_A), vec_spec(HEAD_DIM_A), vec_spec(HEAD_DIM_A), vec_spec(HEAD_DIM_A),
                  vec_spec(V_DIM_A)],
        out_specs=pl.BlockSpec((1, s, V_DIM_A), lambda bi, hi: (hi, bi, 0)),
        out_shape=jax.ShapeDtypeStruct((N_HEADS_A, b * s, V_DIM_A), bf16),
        scratch_shapes=[pltpu.VMEM((s, HEAD_DIM_A + LANES), bf16),
                        pltpu.VMEM((s, HEAD_DIM_A + LANES), bf16),
                        pltpu.VMEM((bq // bk, bq, bk), f32),
                        pltpu.VMEM((bq, bk), f32), pltpu.VMEM((bq, bk), f32),
                        pltpu.VMEM((bq, bk), bf16), pltpu.VMEM((bq, bk), bf16),
                        pltpu.VMEM((bq, V_DIM_A), f32), pltpu.VMEM((bq, V_DIM_A), f32),
                        stat, stat, stat, stat, stat,
                        pltpu.VMEM((bq, HEAD_DIM_A + LANES), bf16),
                        pltpu.VMEM((bq, HEAD_DIM_A + LANES), bf16)],
        compiler_params=_params(("parallel", "arbitrary")),
        name="diff_attention",
    )(qkh, qkh, qkh, qkh, vh, row(lq1), row(lk1), row(lq2), row(lk2), row(dog))


def _gate_kernel(u_ref, v_ref, g0_ref, g1_ref, ya_ref, gv_ref, ws_ref, bt_ref, o_ref, *, width):
    f32 = jnp.float32
    gd = width // N_GROUPS_B
    heads_per_group = gd // V_DIM_A

    v = jax.nn.gelu(v_ref[...].astype(f32))
    ms = jnp.mean(v * v, axis=-1, keepdims=True)
    vn = (v * lax.rsqrt(ms + EPS) * gv_ref[...]).astype(jnp.bfloat16)
    i = lax.broadcasted_iota(jnp.int32, (GMLP_CHUNK, GMLP_CHUNK), 0)
    j = lax.broadcasted_iota(jnp.int32, (GMLP_CHUNK, GMLP_CHUNK), 1)
    allowed = (j // CHUNK) <= (i // CHUNK)
    for g in range(N_GROUPS_B):
        wm = jnp.where(allowed, ws_ref[g], 0.0).astype(jnp.bfloat16)
        bias = bt_ref[:, g:g + 1]
        cols = slice(g * gd, (g + 1) * gd)
        for wdw in range(GATE_ROWS // GMLP_CHUNK):
            rows = slice(wdw * GMLP_CHUNK, (wdw + 1) * GMLP_CHUNK)
            mixed = jnp.dot(wm, vn[rows, cols], preferred_element_type=f32) + bias
            y_b = jax.nn.gelu(u_ref[rows, cols].astype(f32)) * mixed
            y_a = jnp.concatenate([ya_ref[g * heads_per_group + t, rows, :]
                                   for t in range(heads_per_group)], axis=1).astype(f32)
            merged = (jax.nn.sigmoid(g0_ref[rows, cols].astype(f32)) * y_a
                      + jax.nn.sigmoid(g1_ref[rows, cols].astype(f32)) * y_b)
            o_ref[rows, cols] = merged.astype(o_ref.dtype)


def _gate_merge(zr, yah, gv, ws, b, width):
    m = zr.shape[0]
    zspec = lambda blk: pl.BlockSpec((GATE_ROWS, width), lambda i: (i, blk))
    return pl.pallas_call(
        functools.partial(_gate_kernel, width=width),
        grid=(m // GATE_ROWS,),
        in_specs=[zspec(0), zspec(1), zspec(2), zspec(3),
                  pl.BlockSpec((N_HEADS_A, GATE_ROWS, V_DIM_A), lambda i: (0, i, 0)),
                  pl.BlockSpec((1, width), lambda i: (0, 0)),
                  pl.BlockSpec((N_GROUPS_B, GMLP_CHUNK, GMLP_CHUNK), lambda i: (0, 0, 0)),
                  pl.BlockSpec((GMLP_CHUNK, N_GROUPS_B), lambda i: (0, 0))],
        out_specs=pl.BlockSpec((GATE_ROWS, width), lambda i: (i, 0)),
        out_shape=jax.ShapeDtypeStruct((m, width), jnp.bfloat16),
        compiler_params=_params(("parallel",)),
        name="gmlp_gate_merge",
    )(zr, zr, zr, zr, yah, gv.reshape(1, width), ws, b.T)


def _lambda_init(layer_idx):
    return 0.8 - 0.6 * math.exp(-0.3 * (layer_idx - 1))


def kernel(x, norm_mix_g, w_in, qk_gain_q, qk_gain_k, lam_q1, lam_k1, lam_q2, lam_k2, diff_out_g,
           gmlp_v_g, gmlp_ws, gmlp_b, w_o, norm_mlp_g, w_up, w_down):
    b, s, d = x.shape
    depth = w_in.shape[0]
    qk_cols = N_HEADS_A * HEAD_DIM_A
    qkv_end = 4 * qk_cols
    v_cols = N_HEADS_A * V_DIM_A
    qscale = HEAD_DIM_A ** -0.5 * LOG2E
    xf = x.reshape(b * s, d)
    for l in range(depth):
        h = _rmsnorm(xf, norm_mix_g[l])
        gq = jnp.tile(qk_gain_q[l] * qscale, 2 * N_HEADS_A)
        gk = jnp.tile(qk_gain_k[l], 2 * N_HEADS_A)
        qkh = _wcast_matmul(h, w_in, l, col_off=0, ncols=qkv_end, epilogue="headnorm",
                            colgain=jnp.concatenate([gq, gk]), head_width=HEAD_DIM_A)
        vh = _wcast_matmul(h, w_in, l, col_off=qkv_end, ncols=v_cols, epilogue="heads",
                           head_width=V_DIM_A)
        zr = _wcast_matmul(h, w_in, l, col_off=qkv_end + v_cols)
        yah = _attention(qkh, vh, b, lam_q1[l], lam_k1[l], lam_q2[l], lam_k2[l], diff_out_g[l],
                         _lambda_init(l + 1))
        merged = _gate_merge(zr, yah, gmlp_v_g[l], gmlp_ws[l], gmlp_b[l], d)
        xf = _wcast_matmul(merged, w_o, l, bn=MM_BN // 2, epilogue="residual", residual=xf,
                           out_dtype=jnp.float32)
        h2 = _rmsnorm(xf, norm_mlp_g[l])
        a, w_down_bf16 = _wcast_matmul(h2, w_up, l, epilogue="relu2", side=w_down)
        xf = _ksplit_matmul(a, w_down_bf16, xf)
    return xf.reshape(b, s, d)
```

```python
import functools
import math

import jax
import jax.numpy as jnp
from jax import lax
from jax.experimental import pallas as pl
from jax.experimental.pallas import tpu as pltpu

CHUNK = 64
N_HEADS_A = 16
HEAD_DIM_A = 128
V_DIM_A = 2 * HEAD_DIM_A
GMLP_CHUNK = 128
N_GROUPS_B = 8
EPS = 1e-6
MASK_VALUE = -1e30
LOG2E = math.log2(math.e)

V7X_VMEM_BYTES = 64 * 1024 * 1024
VMEM_LIMIT_BYTES = V7X_VMEM_BYTES - 8 * 1024 * 1024
LANES = 128

MM_BM = 1024
MM_BN = 1024
MM_BK = 4096
NORM_ROWS = 256
ATT_BQ = 1024
ATT_BK = 512
ATT_STRIP = 64
GATE_ROWS = 256


def _params(semantics):
    return pltpu.CompilerParams(dimension_semantics=semantics, vmem_limit_bytes=VMEM_LIMIT_BYTES)


def _rmsnorm_kernel(x_ref, g_ref, o_ref):
    x = x_ref[...]
    ms = jnp.mean(x * x, axis=-1, keepdims=True)
    o_ref[...] = (x * lax.rsqrt(ms + EPS) * g_ref[...]).astype(o_ref.dtype)


def _rmsnorm(x, g):
    m, d = x.shape
    return pl.pallas_call(
        _rmsnorm_kernel,
        grid=(m // NORM_ROWS,),
        in_specs=[pl.BlockSpec((NORM_ROWS, d), lambda i: (i, 0)),
                  pl.BlockSpec((1, d), lambda i: (0, 0))],
        out_specs=pl.BlockSpec((NORM_ROWS, d), lambda i: (i, 0)),
        out_shape=jax.ShapeDtypeStruct((m, d), jnp.bfloat16),
        compiler_params=_params(("parallel",)),
        name="rmsnorm",
    )(x, g.reshape(1, d))


def _wcast_matmul_kernel(*refs, epilogue, has_side, chunk, head_width):
    refs = list(refs)
    x_ref, w_ref = refs[0], refs[1]
    pos = 2
    r_ref = g_ref = side_in = side_out = None
    if epilogue == "residual":
        r_ref = refs[pos]; pos += 1
    if epilogue == "headnorm":
        g_ref = refs[pos]; pos += 1
    if has_side:
        side_in = refs[pos]; pos += 1
    o_ref = refs[pos]; pos += 1
    if has_side:
        side_out = refs[pos]; pos += 1
    panel0, panel1 = refs[pos], refs[pos + 1]

    n = pl.program_id(0)
    m = pl.program_id(1)
    rows = pl.ds(pl.multiple_of(m * chunk, chunk), chunk)

    def cast_chunk(panel):
        panel[rows, :] = w_ref[...].astype(panel.dtype)

    def compute(panel):
        acc = jnp.dot(x_ref[...], panel[...], preferred_element_type=jnp.float32)
        if epilogue in ("headnorm", "heads"):
            for t in range(acc.shape[1] // head_width):
                cols = slice(t * head_width, (t + 1) * head_width)
                xh = acc[:, cols]
                if epilogue == "headnorm":
                    ms = jnp.mean(xh * xh, axis=-1, keepdims=True)
                    xh = xh * lax.rsqrt(ms + EPS) * g_ref[:, cols]
                o_ref[t] = xh.astype(o_ref.dtype)
        else:
            if epilogue == "relu2":
                acc = jnp.square(jnp.maximum(acc, 0.0))
            elif epilogue == "residual":
                acc = r_ref[...] + acc
            o_ref[...] = acc.astype(o_ref.dtype)
        if has_side:
            side_out[...] = side_in[...].astype(side_out.dtype)

    @pl.when(n == 0)
    def _():
        cast_chunk(panel0)

    @pl.when(jnp.logical_and(n > 0, n % 2 == 1))
    def _():
        cast_chunk(panel1)
        compute(panel0)

    @pl.when(jnp.logical_and(n > 0, n % 2 == 0))
    def _():
        cast_chunk(panel0)
        compute(panel1)


def _wcast_matmul(x, w, layer, *, col_off=0, ncols=None, bn=MM_BN, epilogue="none", residual=None,
                  colgain=None, head_width=None, out_dtype=jnp.bfloat16, side=None):
    m, kdim = x.shape
    n = w.shape[2] - col_off if ncols is None else ncols
    bm = min(MM_BM, m)
    nm, nn = m // bm, n // bn
    chunk = kdim // nm
    col0 = col_off // bn
    row_blk = lambda ni, mi: jnp.where(ni > 0, mi, 0)
    col_blk = lambda ni: jnp.maximum(ni - 1, 0)
    in_specs = [pl.BlockSpec((bm, kdim), lambda ni, mi: (row_blk(ni, mi), 0)),
                pl.BlockSpec((None, chunk, bn),
                             lambda ni, mi: (layer, jnp.where(ni < nn, mi, nm - 1),
                                             col0 + jnp.minimum(ni, nn - 1)))]
    args = [x, w]
    if epilogue == "residual":
        in_specs.append(pl.BlockSpec((bm, bn), lambda ni, mi: (row_blk(ni, mi), col_blk(ni))))
        args.append(residual)
    if epilogue == "headnorm":
        in_specs.append(pl.BlockSpec((1, bn), lambda ni, mi: (0, col_blk(ni))))
        args.append(colgain.reshape(1, n))
    if head_width is None:
        out_specs = [pl.BlockSpec((bm, bn), lambda ni, mi: (row_blk(ni, mi), col_blk(ni)))]
        out_shape = [jax.ShapeDtypeStruct((m, n), out_dtype)]
    else:
        out_specs = [pl.BlockSpec((bn // head_width, bm, head_width),
                                  lambda ni, mi: (col_blk(ni), row_blk(ni, mi), 0))]
        out_shape = [jax.ShapeDtypeStruct((n // head_width, m, head_width), out_dtype)]
    if side is not None:
        _, srows, scols = side.shape
        sblk = srows // (nn * nm)
        side_idx = lambda ni, mi: (col_blk(ni) * nm + row_blk(ni, mi), 0)
        in_specs.append(pl.BlockSpec((None, sblk, scols), lambda ni, mi: (layer,) + side_idx(ni, mi)))
        args.append(side)
        out_specs.append(pl.BlockSpec((sblk, scols), side_idx))
        out_shape.append(jax.ShapeDtypeStruct((srows, scols), jnp.bfloat16))
    outs = pl.pallas_call(
        functools.partial(_wcast_matmul_kernel, epilogue=epilogue, has_side=side is not None,
                          chunk=chunk, head_width=head_width),
        grid=(nn + 1, nm),
        in_specs=in_specs,
        out_specs=out_specs,
        out_shape=out_shape,
        scratch_shapes=[pltpu.VMEM((kdim, bn), jnp.bfloat16), pltpu.VMEM((kdim, bn), jnp.bfloat16)],
        compiler_params=_params(("arbitrary", "arbitrary")),
        name="wcast_matmul_" + epilogue,
    )(*args)
    return outs if side is not None else outs[0]


def _ksplit_matmul_kernel(x_ref, w_ref, r_ref, o_ref):
    k = pl.program_id(2)

    @pl.when(k == 0)
    def _():
        o_ref[...] = r_ref[...] + jnp.dot(x_ref[...], w_ref[...], preferred_element_type=jnp.float32)

    @pl.when(k > 0)
    def _():
        o_ref[...] += jnp.dot(x_ref[...], w_ref[...], preferred_element_type=jnp.float32)


def _ksplit_matmul(x, w, residual):
    m, kdim = x.shape
    _, n = w.shape
    bm, bn, bk = min(MM_BM, m), min(MM_BN, n), min(MM_BK, kdim)
    return pl.pallas_call(
        _ksplit_matmul_kernel,
        grid=(m // bm, n // bn, kdim // bk),
        in_specs=[pl.BlockSpec((bm, bk), lambda i, j, k: (i, k)),
                  pl.BlockSpec((bk, bn), lambda i, j, k: (k, j)),
                  pl.BlockSpec((bm, bn), lambda i, j, k: (i, j))],
        out_specs=pl.BlockSpec((bm, bn), lambda i, j, k: (i, j)),
        out_shape=jax.ShapeDtypeStruct((m, n), jnp.float32),
        compiler_params=_params(("parallel", "parallel", "arbitrary")),
        name="ksplit_matmul_residual",
    )(x, w, residual)


def _attention_kernel(q1_ref, q2_ref, k1_ref, k2_ref, v_ref,
                      lq1_ref, lk1_ref, lq2_ref, lk2_ref, dog_ref, o_ref,
                      kn1_ref, kn2_ref, dbias_ref, s1_ref, s2_ref, p1_ref, p2_ref,
                      acc1_ref, acc2_ref, m1_ref, l1_ref, a1_ref, m2_ref, l2_ref, q1a_ref, q2a_ref,
                      *, seq, lam_init):
    h = pl.program_id(1)
    bq, bk = ATT_BQ, ATT_BK
    f32, bf16 = jnp.float32, jnp.bfloat16

    slope = jnp.exp2(jnp.full((1, 1), -8.0 / N_HEADS_A, f32) * (h + 1).astype(f32)) * LOG2E
    lane = lax.broadcasted_iota(jnp.int32, (bk, LANES), 1)

    def key_chunk(i, _):
        rows = pl.ds(pl.multiple_of(i * bk, bk), bk)
        pos = (lax.broadcasted_iota(jnp.int32, (bk, 1), 0) + i * bk).astype(f32)
        bias = slope * pos
        hi = bias.astype(bf16).astype(f32)
        mid = (bias - hi).astype(bf16).astype(f32)
        lo = (bias - hi - mid).astype(bf16).astype(f32)
        extra = jnp.where(lane == 0, hi, jnp.where(lane == 1, mid, jnp.where(lane == 2, lo, 0.0)))
        extra = extra.astype(bf16)
        kn1_ref[rows, :HEAD_DIM_A] = k1_ref[0, rows, :]
        kn2_ref[rows, :HEAD_DIM_A] = k2_ref[0, rows, :]
        kn1_ref[rows, HEAD_DIM_A:] = extra
        kn2_ref[rows, HEAD_DIM_A:] = extra
        return 0

    lax.fori_loop(0, seq // bk, key_chunk, 0)
    qlane = lax.broadcasted_iota(jnp.int32, (bq, LANES), 1)
    ones = jnp.where(qlane < 3, 1.0, 0.0).astype(bf16)
    q1a_ref[:, HEAD_DIM_A:] = ones
    q2a_ref[:, HEAD_DIM_A:] = ones
    n_diag = bq // bk
    for d in range(n_diag):
        r = lax.broadcasted_iota(jnp.int32, (bq, bk), 0)
        c = lax.broadcasted_iota(jnp.int32, (bq, bk), 1) + d * bk
        future = jnp.maximum(c - r, 0).astype(f32)
        dbias_ref[d] = jnp.where((c // CHUNK) <= (r // CHUNK), -2.0 * slope * future, MASK_VALUE)

    lam = (jnp.exp(jnp.sum(lq1_ref[...] * lk1_ref[...], keepdims=True))
           - jnp.exp(jnp.sum(lq2_ref[...] * lk2_ref[...], keepdims=True)) + lam_init)

    def scores(q, k):
        return lax.dot_general(q, k, (((1,), (1,)), ((), ())), preferred_element_type=f32)

    def softmax(s_ref, p_ref, m_ref, l_ref, diag, row0):
        live = slice(row0, bq)
        m_prev = m_ref[live, :]
        s_all = s_ref[live, :] if diag is None else s_ref[live, :] + dbias_ref[diag, live, :]
        m_next = jnp.maximum(m_prev, jnp.max(s_all, axis=1, keepdims=True))
        alpha = jnp.exp2(m_prev - m_next)
        m_ref[live, :] = m_next
        for t in range((bq - row0) // ATT_STRIP):
            part = slice(t * ATT_STRIP, (t + 1) * ATT_STRIP)
            rows = slice(row0 + t * ATT_STRIP, row0 + (t + 1) * ATT_STRIP)
            s = s_ref[rows, :] if diag is None else s_ref[rows, :] + dbias_ref[diag, rows, :]
            p = jnp.exp2(s - jnp.tile(m_next[part, :], (1, bk // LANES)))
            psum = p[:, :LANES]
            for j in range(1, bk // LANES):
                psum = psum + p[:, j * LANES:(j + 1) * LANES]
            l_ref[rows, :] = alpha[part, :] * l_ref[rows, :] + psum
            p_ref[rows, :] = p.astype(p_ref.dtype)
        return alpha

    def accumulate(acc_ref, alpha, p_ref, v, row0):
        live = slice(row0, bq)
        acc_ref[live, :] = (jnp.tile(alpha, (1, V_DIM_A // LANES)) * acc_ref[live, :]
                            + jnp.dot(p_ref[live, :], v, preferred_element_type=f32))

    def key_rows(kj):
        return pl.ds(pl.multiple_of(kj * bk, bk), bk)

    def reset_state():
        for ref in (acc1_ref, acc2_ref, l1_ref, l2_ref, p1_ref):
            ref[...] = jnp.zeros_like(ref)
        m1_ref[...] = jnp.full_like(m1_ref, MASK_VALUE)
        m2_ref[...] = jnp.full_like(m2_ref, MASK_VALUE)
        a1_ref[...] = jnp.ones_like(a1_ref)

    def query_block(qi, _):
        qrows = pl.ds(pl.multiple_of(qi * bq, bq), bq)
        q1a_ref[:, :HEAD_DIM_A] = q1_ref[0, qrows, :]
        q2a_ref[:, :HEAD_DIM_A] = q2_ref[0, qrows, :]

        s2_ref[...] = scores(q2a_ref[...], kn2_ref[key_rows(0), :])
        reset_state()

        def stage(kb, diag, prev_row0, row0, next_row0):
            accumulate(acc1_ref, a1_ref[prev_row0:, :], p1_ref,
                       v_ref[0, key_rows(jnp.maximum(kb - 1, 0)), :], prev_row0)
            alpha2 = softmax(s2_ref, p2_ref, m2_ref, l2_ref, diag, row0)
            s1_ref[row0:, :] = scores(q1a_ref[row0:, :], kn1_ref[key_rows(kb), :])
            a1_ref[row0:, :] = softmax(s1_ref, p1_ref, m1_ref, l1_ref, diag, row0)
            accumulate(acc2_ref, alpha2, p2_ref, v_ref[0, key_rows(kb), :], row0)
            if next_row0 is not None:
                s2_ref[next_row0:, :] = scores(q2a_ref[next_row0:, :], kn2_ref[key_rows(kb + 1), :])

        def past_block(kj, _):
            stage(kj, None, 0, 0, 0)
            return 0

        first_diag = qi * n_diag
        lax.fori_loop(0, first_diag, past_block, 0)
        for d in range(n_diag):
            stage(first_diag + d, d, max(d - 1, 0) * bk, d * bk,
                  (d + 1) * bk if d + 1 < n_diag else None)
        last_row0 = (n_diag - 1) * bk
        accumulate(acc1_ref, a1_ref[last_row0:, :], p1_ref,
                   v_ref[0, key_rows(first_diag + n_diag - 1), :], last_row0)

        w1 = 1.0 / jnp.sum(l1_ref[...], axis=1, keepdims=True)
        w2 = lam / jnp.sum(l2_ref[...], axis=1, keepdims=True)
        o = acc1_ref[...] * w1 - acc2_ref[...] * w2
        ms = jnp.mean(o * o, axis=-1, keepdims=True)
        y = o * lax.rsqrt(ms + EPS) * dog_ref[...] * (1.0 - lam_init)
        o_ref[0, qrows, :] = y.astype(o_ref.dtype)
        return 0

    lax.fori_loop(0, seq // bq, query_block, 0)


def _attention(qkh, vh, b, lq1, lk1, lq2, lk2, dog, lam_init):
    s = qkh.shape[1] // b
    bq, bk = ATT_BQ, ATT_BK
    head_spec = lambda w, off: pl.BlockSpec((1, s, w), lambda bi, hi: (off + hi, bi, 0))
    vec_spec = lambda w: pl.BlockSpec((1, w), lambda bi, hi: (0, 0))
    row = lambda a: a.reshape(1, -1)
    f32, bf16 = jnp.float32, jnp.bfloat16
    stat = pltpu.VMEM((bq, LANES), f32)
    return pl.pallas_call(
        functools.partial(_attention_kernel, seq=s, lam_init=lam_init),
        grid=(b, N_HEADS_A),
        in_specs=[head_spec(HEAD_DIM_A, 0), head_spec(HEAD_DIM_A, N_HEADS_A),
                  head_spec(HEAD_DIM_A, 2 * N_HEADS_A), head_spec(HEAD_DIM_A, 3 * N_HEADS_A),
                  head_spec(V_DIM_A, 0),
                  vec_spec(HEAD_DIM_A), vec_spec(HEAD_DIM_A), vec_spec(HEAD_DIM_A), vec_spec(HEAD_DIM_A),
                  vec_spec(V_DIM_A)],
        out_specs=pl.BlockSpec((1, s, V_DIM_A), lambda bi, hi: (hi, bi, 0)),
        out_shape=jax.ShapeDtypeStruct((N_HEADS_A, b * s, V_DIM_A), bf16),
        scratch_shapes=[pltpu.VMEM((s, HEAD_DIM_A + LANES), bf16),
                        pltpu.VMEM((s, HEAD_DIM_A + LANES), bf16),
                        pltpu.VMEM((bq // bk, bq, bk), f32),
                        pltpu.VMEM((bq, bk), f32), pltpu.VMEM((bq, bk), f32),
                        pltpu.VMEM((bq, bk), bf16), pltpu.VMEM((bq, bk), bf16),
                        pltpu.VMEM((bq, V_DIM_A), f32), pltpu.VMEM((bq, V_DIM_A), f32),
                        stat, stat, stat, stat, stat,
                        pltpu.VMEM((bq, HEAD_DIM_A + LANES), bf16),
                        pltpu.VMEM((bq, HEAD_DIM_A + LANES), bf16)],
        compiler_params=_params(("parallel", "arbitrary")),
        name="diff_attention",
    )(qkh, qkh, qkh, qkh, vh, row(lq1), row(lk1), row(lq2), row(lk2), row(dog))


def _gate_kernel(u_ref, v_ref, g0_ref, g1_ref, ya_ref, gv_ref, ws_ref, bt_ref, o_ref, *, width):
    f32 = jnp.float32
    gd = width // N_GROUPS_B
    heads_per_group = gd // V_DIM_A

    v = jax.nn.gelu(v_ref[...].astype(f32))
    ms = jnp.mean(v * v, axis=-1, keepdims=True)
    vn = (v * lax.rsqrt(ms + EPS) * gv_ref[...]).astype(jnp.bfloat16)
    i = lax.broadcasted_iota(jnp.int32, (GMLP_CHUNK, GMLP_CHUNK), 0)
    j = lax.broadcasted_iota(jnp.int32, (GMLP_CHUNK, GMLP_CHUNK), 1)
    allowed = (j // CHUNK) <= (i // CHUNK)
    for g in range(N_GROUPS_B):
        wm = jnp.where(allowed, ws_ref[g], 0.0).astype(jnp.bfloat16)
        bias = bt_ref[:, g:g + 1]
        cols = slice(g * gd, (g + 1) * gd)
        for wdw in range(GATE_ROWS // GMLP_CHUNK):
            rows = slice(wdw * GMLP_CHUNK, (wdw + 1) * GMLP_CHUNK)
            mixed = jnp.dot(wm, vn[rows, cols], preferred_element_type=f32) + bias
            y_b = jax.nn.gelu(u_ref[rows, cols].astype(f32)) * mixed
            y_a = jnp.concatenate([ya_ref[g * heads_per_group + t, rows, :]
                                   for t in range(heads_per_group)], axis=1).astype(f32)
            merged = (jax.nn.sigmoid(g0_ref[rows, cols].astype(f32)) * y_a
                      + jax.nn.sigmoid(g1_ref[rows, cols].astype(f32)) * y_b)
            o_ref[rows, cols] = merged.astype(o_ref.dtype)


def _gate_merge(zr, yah, gv, ws, b, width):
    m = zr.shape[0]
    zspec = lambda blk: pl.BlockSpec((GATE_ROWS, width), lambda i: (i, blk))
    return pl.pallas_call(
        functools.partial(_gate_kernel, width=width),
        grid=(m // GATE_ROWS,),
        in_specs=[zspec(0), zspec(1), zspec(2), zspec(3),
                  pl.BlockSpec((N_HEADS_A, GATE_ROWS, V_DIM_A), lambda i: (0, i, 0)),
                  pl.BlockSpec((1, width), lambda i: (0, 0)),
                  pl.BlockSpec((N_GROUPS_B, GMLP_CHUNK, GMLP_CHUNK), lambda i: (0, 0, 0)),
                  pl.BlockSpec((GMLP_CHUNK, N_GROUPS_B), lambda i: (0, 0))],
        out_specs=pl.BlockSpec((GATE_ROWS, width), lambda i: (i, 0)),
        out_shape=jax.ShapeDtypeStruct((m, width), jnp.bfloat16),
        compiler_params=_params(("parallel",)),
        name="gmlp_gate_merge",
    )(zr, zr, zr, zr, yah, gv.reshape(1, width), ws, b.T)


def _lambda_init(layer_idx):
    return 0.8 - 0.6 * math.exp(-0.3 * (layer_idx - 1))


def kernel(x, norm_mix_g, w_in, qk_gain_q, qk_gain_k, lam_q1, lam_k1, lam_q2, lam_k2, diff_out_g,
           gmlp_v_g, gmlp_ws, gmlp_b, w_o, norm_mlp_g, w_up, w_down):
    b, s, d = x.shape
    depth = w_in.shape[0]
    qk_cols = N_HEADS_A * HEAD_DIM_A
    qkv_end = 4 * qk_cols
    v_cols = N_HEADS_A * V_DIM_A
    qscale = HEAD_DIM_A ** -0.5 * LOG2E
    xf = x.reshape(b * s, d)
    for l in range(depth):
        h = _rmsnorm(xf, norm_mix_g[l])
        gq = jnp.tile(qk_gain_q[l] * qscale, 2 * N_HEADS_A)
        gk = jnp.tile(qk_gain_k[l], 2 * N_HEADS_A)
        qkh = _wcast_matmul(h, w_in, l, col_off=0, ncols=qkv_end, epilogue="headnorm",
                            colgain=jnp.concatenate([gq, gk]), head_width=HEAD_DIM_A)
        vh = _wcast_matmul(h, w_in, l, col_off=qkv_end, ncols=v_cols, epilogue="heads",
                           head_width=V_DIM_A)
        zr = _wcast_matmul(h, w_in, l, col_off=qkv_end + v_cols)
        yah = _attention(qkh, vh, b, lam_q1[l], lam_k1[l], lam_q2[l], lam_k2[l], diff_out_g[l],
                         _lambda_init(l + 1))
        merged = _gate_merge(zr, yah, gmlp_v_g[l], gmlp_ws[l], gmlp_b[l], d)
        xf = _wcast_matmul(merged, w_o, l, bn=MM_BN // 2, epilogue="residual", residual=xf,
                           out_dtype=jnp.float32)
        h2 = _rmsnorm(xf, norm_mlp_g[l])
        a, w_down_bf16 = _wcast_matmul(h2, w_up, l, epilogue="relu2", side=w_down)
        xf = _ksplit_matmul(a, w_down_bf16, xf)
    return xf.reshape(b, s, d)
```

```python
import functools
import math

import jax
import jax.numpy as jnp
from jax import lax
from jax.experimental import pallas as pl
from jax.experimental.pallas import tpu as pltpu

CHUNK = 64
N_HEADS_A = 16
HEAD_DIM_A = 128
V_DIM_A = 2 * HEAD_DIM_A
GMLP_CHUNK = 128
N_GROUPS_B = 8
EPS = 1e-6
MASK_VALUE = -1e30
LOG2E = math.log2(math.e)

V7X_VMEM_BYTES = 64 * 1024 * 1024
VMEM_LIMIT_BYTES = V7X_VMEM_BYTES - 8 * 1024 * 1024
LANES = 128

MM_BM = 1024
MM_BN = 1024
MM_BK = 4096
NORM_ROWS = 256
ATT_BQ = 1024
ATT_BK = 256
ATT_STRIP = 64
GATE_ROWS = 256


def _params(semantics):
    return pltpu.CompilerParams(dimension_semantics=semantics, vmem_limit_bytes=VMEM_LIMIT_BYTES)


def _rmsnorm_kernel(x_ref, g_ref, o_ref):
    x = x_ref[...]
    ms = jnp.mean(x * x, axis=-1, keepdims=True)
    o_ref[...] = (x * lax.rsqrt(ms + EPS) * g_ref[...]).astype(o_ref.dtype)


def _rmsnorm(x, g):
    m, d = x.shape
    return pl.pallas_call(
        _rmsnorm_kernel,
        grid=(m // NORM_ROWS,),
        in_specs=[pl.BlockSpec((NORM_ROWS, d), lambda i: (i, 0)),
                  pl.BlockSpec((1, d), lambda i: (0, 0))],
        out_specs=pl.BlockSpec((NORM_ROWS, d), lambda i: (i, 0)),
        out_shape=jax.ShapeDtypeStruct((m, d), jnp.bfloat16),
        compiler_params=_params(("parallel",)),
        name="rmsnorm",
    )(x, g.reshape(1, d))


def _wcast_matmul_kernel(*refs, epilogue, has_side, chunk, head_width):
    refs = list(refs)
    x_ref, w_ref = refs[0], refs[1]
    pos = 2
    r_ref = g_ref = side_in = side_out = None
    if epilogue == "residual":
        r_ref = refs[pos]; pos += 1
    if epilogue == "headnorm":
        g_ref = refs[pos]; pos += 1
    if has_side:
        side_in = refs[pos]; pos += 1
    o_ref = refs[pos]; pos += 1
    if has_side:
        side_out = refs[pos]; pos += 1
    panel0, panel1 = refs[pos], refs[pos + 1]

    n = pl.program_id(0)
    m = pl.program_id(1)
    rows = pl.ds(pl.multiple_of(m * chunk, chunk), chunk)

    def cast_chunk(panel):
        panel[rows, :] = w_ref[...].astype(panel.dtype)

    def compute(panel):
        acc = jnp.dot(x_ref[...], panel[...], preferred_element_type=jnp.float32)
        if epilogue in ("headnorm", "heads"):
            for t in range(acc.shape[1] // head_width):
                cols = slice(t * head_width, (t + 1) * head_width)
                xh = acc[:, cols]
                if epilogue == "headnorm":
                    ms = jnp.mean(xh * xh, axis=-1, keepdims=True)
                    xh = xh * lax.rsqrt(ms + EPS) * g_ref[:, cols]
                o_ref[t] = xh.astype(o_ref.dtype)
        else:
            if epilogue == "relu2":
                acc = jnp.square(jnp.maximum(acc, 0.0))
            elif epilogue == "residual":
                acc = r_ref[...] + acc
            o_ref[...] = acc.astype(o_ref.dtype)
        if has_side:
            side_out[...] = side_in[...].astype(side_out.dtype)

    @pl.when(n == 0)
    def _():
        cast_chunk(panel0)

    @pl.when(jnp.logical_and(n > 0, n % 2 == 1))
    def _():
        cast_chunk(panel1)
        compute(panel0)

    @pl.when(jnp.logical_and(n > 0, n % 2 == 0))
    def _():
        cast_chunk(panel0)
        compute(panel1)


def _wcast_matmul(x, w, layer, *, col_off=0, ncols=None, bn=MM_BN, epilogue="none", residual=None,
                  colgain=None, head_width=None, out_dtype=jnp.bfloat16, side=None):
    m, kdim = x.shape
    n = w.shape[2] - col_off if ncols is None else ncols
    bm = min(MM_BM, m)
    nm, nn = m // bm, n // bn
    chunk = kdim // nm
    col0 = col_off // bn
    row_blk = lambda ni, mi: jnp.where(ni > 0, mi, 0)
    col_blk = lambda ni: jnp.maximum(ni - 1, 0)
    in_specs = [pl.BlockSpec((bm, kdim), lambda ni, mi: (row_blk(ni, mi), 0)),
                pl.BlockSpec((None, chunk, bn),
                             lambda ni, mi: (layer, jnp.where(ni < nn, mi, nm - 1),
                                             col0 + jnp.minimum(ni, nn - 1)))]
    args = [x, w]
    if epilogue == "residual":
        in_specs.append(pl.BlockSpec((bm, bn), lambda ni, mi: (row_blk(ni, mi), col_blk(ni))))
        args.append(residual)
    if epilogue == "headnorm":
        in_specs.append(pl.BlockSpec((1, bn), lambda ni, mi: (0, col_blk(ni))))
        args.append(colgain.reshape(1, n))
    if head_width is None:
        out_specs = [pl.BlockSpec((bm, bn), lambda ni, mi: (row_blk(ni, mi), col_blk(ni)))]
        out_shape = [jax.ShapeDtypeStruct((m, n), out_dtype)]
    else:
        out_specs = [pl.BlockSpec((bn // head_width, bm, head_width),
                                  lambda ni, mi: (col_blk(ni), row_blk(ni, mi), 0))]
        out_shape = [jax.ShapeDtypeStruct((n // head_width, m, head_width), out_dtype)]
    if side is not None:
        _, srows, scols = side.shape
        sblk = srows // (nn * nm)
        side_idx = lambda ni, mi: (col_blk(ni) * nm + row_blk(ni, mi), 0)
        in_specs.append(pl.BlockSpec((None, sblk, scols), lambda ni, mi: (layer,) + side_idx(ni, mi)))
        args.append(side)
        out_specs.append(pl.BlockSpec((sblk, scols), side_idx))
        out_shape.append(jax.ShapeDtypeStruct((srows, scols), jnp.bfloat16))
    outs = pl.pallas_call(
        functools.partial(_wcast_matmul_kernel, epilogue=epilogue, has_side=side is not None,
                          chunk=chunk, head_width=head_width),
        grid=(nn + 1, nm),
        in_specs=in_specs,
        out_specs=out_specs,
        out_shape=out_shape,
        scratch_shapes=[pltpu.VMEM((kdim, bn), jnp.bfloat16), pltpu.VMEM((kdim, bn), jnp.bfloat16)],
        compiler_params=_params(("arbitrary", "arbitrary")),
        name="wcast_matmul_" + epilogue,
    )(*args)
    return outs if side is not None else outs[0]


def _ksplit_matmul_kernel(x_ref, w_ref, r_ref, o_ref):
    k = pl.program_id(2)

    @pl.when(k == 0)
    def _():
        o_ref[...] = r_ref[...] + jnp.dot(x_ref[...], w_ref[...], preferred_element_type=jnp.float32)

    @pl.when(k > 0)
    def _():
        o_ref[...] += jnp.dot(x_ref[...], w_ref[...], preferred_element_type=jnp.float32)


def _ksplit_matmul(x, w, residual):
    m, kdim = x.shape
    _, n = w.shape
    bm, bn, bk = min(MM_BM, m), min(MM_BN, n), min(MM_BK, kdim)
    return pl.pallas_call(
        _ksplit_matmul_kernel,
        grid=(m // bm, n // bn, kdim // bk),
        in_specs=[pl.BlockSpec((bm, bk), lambda i, j, k: (i, k)),
                  pl.BlockSpec((bk, bn), lambda i, j, k: (k, j)),
                  pl.BlockSpec((bm, bn), lambda i, j, k: (i, j))],
        out_specs=pl.BlockSpec((bm, bn), lambda i, j, k: (i, j)),
        out_shape=jax.ShapeDtypeStruct((m, n), jnp.float32),
        compiler_params=_params(("parallel", "parallel", "arbitrary")),
        name="ksplit_matmul_residual",
    )(x, w, residual)


def _attention_kernel(q1_ref, q2_ref, k1_ref, k2_ref, v_ref,
                      lq1_ref, lk1_ref, lq2_ref, lk2_ref, dog_ref, o_ref,
                      kn1_ref, kn2_ref, dbias_ref, s1_ref, s2_ref, p1_ref, p2_ref,
                      acc1_ref, acc2_ref, m1_ref, l1_ref, a1_ref, m2_ref, l2_ref, q1a_ref, q2a_ref,
                      *, seq, lam_init):
    h = pl.program_id(1)
    bq, bk = ATT_BQ, ATT_BK
    f32, bf16 = jnp.float32, jnp.bfloat16

    slope = jnp.exp2(jnp.full((1, 1), -8.0 / N_HEADS_A, f32) * (h + 1).astype(f32)) * LOG2E
    lane = lax.broadcasted_iota(jnp.int32, (bk, LANES), 1)

    def key_chunk(i, _):
        rows = pl.ds(pl.multiple_of(i * bk, bk), bk)
        pos = (lax.broadcasted_iota(jnp.int32, (bk, 1), 0) + i * bk).astype(f32)
        bias = slope * pos
        hi = bias.astype(bf16).astype(f32)
        mid = (bias - hi).astype(bf16).astype(f32)
        lo = (bias - hi - mid).astype(bf16).astype(f32)
        extra = jnp.where(lane == 0, hi, jnp.where(lane == 1, mid, jnp.where(lane == 2, lo, 0.0)))
        extra = extra.astype(bf16)
        kn1_ref[rows, :HEAD_DIM_A] = k1_ref[0, rows, :]
        kn2_ref[rows, :HEAD_DIM_A] = k2_ref[0, rows, :]
        kn1_ref[rows, HEAD_DIM_A:] = extra
        kn2_ref[rows, HEAD_DIM_A:] = extra
        return 0

    lax.fori_loop(0, seq // bk, key_chunk, 0)
    qlane = lax.broadcasted_iota(jnp.int32, (bq, LANES), 1)
    ones = jnp.where(qlane < 3, 1.0, 0.0).astype(bf16)
    q1a_ref[:, HEAD_DIM_A:] = ones
    q2a_ref[:, HEAD_DIM_A:] = ones
    n_diag = bq // bk
    for d in range(n_diag):
        r = lax.broadcasted_iota(jnp.int32, (bq, bk), 0)
        c = lax.broadcasted_iota(jnp.int32, (bq, bk), 1) + d * bk
        future = jnp.maximum(c - r, 0).astype(f32)
        dbias_ref[d] = jnp.where((c // CHUNK) <= (r // CHUNK), -2.0 * slope * future, MASK_VALUE)

    lam = (jnp.exp(jnp.sum(lq1_ref[...] * lk1_ref[...], keepdims=True))
           - jnp.exp(jnp.sum(lq2_ref[...] * lk2_ref[...], keepdims=True)) + lam_init)

    def scores(q, k):
        return lax.dot_general(q, k, (((1,), (1,)), ((), ())), preferred_element_type=f32)

    def softmax(s_ref, p_ref, m_ref, l_ref, diag, row0):
        live = slice(row0, bq)
        m_prev = m_ref[live, :]
        s_all = s_ref[live, :] if diag is None else s_ref[live, :] + dbias_ref[diag, live, :]
        m_next = jnp.maximum(m_prev, jnp.max(s_all, axis=1, keepdims=True))
        alpha = jnp.exp2(m_prev - m_next)
        m_ref[live, :] = m_next
        for t in range((bq - row0) // ATT_STRIP):
            part = slice(t * ATT_STRIP, (t + 1) * ATT_STRIP)
            rows = slice(row0 + t * ATT_STRIP, row0 + (t + 1) * ATT_STRIP)
            s = s_ref[rows, :] if diag is None else s_ref[rows, :] + dbias_ref[diag, rows, :]
            p = jnp.exp2(s - jnp.tile(m_next[part, :], (1, bk // LANES)))
            psum = p[:, :LANES]
            for j in range(1, bk // LANES):
                psum = psum + p[:, j * LANES:(j + 1) * LANES]
            l_ref[rows, :] = alpha[part, :] * l_ref[rows, :] + psum
            p_ref[rows, :] = p.astype(p_ref.dtype)
        return alpha

    def accumulate(acc_ref, alpha, p_ref, v, row0):
        live = slice(row0, bq)
        acc_ref[live, :] = (jnp.tile(alpha, (1, V_DIM_A // LANES)) * acc_ref[live, :]
                            + jnp.dot(p_ref[live, :], v, preferred_element_type=f32))

    def key_rows(kj):
        return pl.ds(pl.multiple_of(kj * bk, bk), bk)

    def reset_state():
        for ref in (acc1_ref, acc2_ref, l1_ref, l2_ref, p1_ref):
            ref[...] = jnp.zeros_like(ref)
        m1_ref[...] = jnp.full_like(m1_ref, MASK_VALUE)
        m2_ref[...] = jnp.full_like(m2_ref, MASK_VALUE)
        a1_ref[...] = jnp.ones_like(a1_ref)

    def query_block(qi, _):
        qrows = pl.ds(pl.multiple_of(qi * bq, bq), bq)
        q1a_ref[:, :HEAD_DIM_A] = q1_ref[0, qrows, :]
        q2a_ref[:, :HEAD_DIM_A] = q2_ref[0, qrows, :]

        s2_ref[...] = scores(q2a_ref[...], kn2_ref[key_rows(0), :])
        reset_state()

        def stage(kb, diag, prev_row0, row0, next_row0):
            accumulate(acc1_ref, a1_ref[prev_row0:, :], p1_ref,
                       v_ref[0, key_rows(jnp.maximum(kb - 1, 0)), :], prev_row0)
            alpha2 = softmax(s2_ref, p2_ref, m2_ref, l2_ref, diag, row0)
            s1_ref[row0:, :] = scores(q1a_ref[row0:, :], kn1_ref[key_rows(kb), :])
            a1_ref[row0:, :] = softmax(s1_ref, p1_ref, m1_ref, l1_ref, diag, row0)
            accumulate(acc2_ref, alpha2, p2_ref, v_ref[0, key_rows(kb), :], row0)
            if next_row0 is not None:
                s2_ref[next_row0:, :] = scores(q2a_ref[next_row0:, :], kn2_ref[key_rows(kb + 1), :])

        def past_block(kj, _):
            stage(kj, None, 0, 0, 0)
            return 0

        first_diag = qi * n_diag
        lax.fori_loop(0, first_diag, past_block, 0)
        for d in range(n_diag):
            stage(first_diag + d, d, max(d - 1, 0) * bk, d * bk,
                  (d + 1) * bk if d + 1 < n_diag else None)
        last_row0 = (n_diag - 1) * bk
        accumulate(acc1_ref, a1_ref[last_row0:, :], p1_ref,
                   v_ref[0, key_rows(first_diag + n_diag - 1), :], last_row0)

        w1 = 1.0 / jnp.sum(l1_ref[...], axis=1, keepdims=True)
        w2 = lam / jnp.sum(l2_ref[...], axis=1, keepdims=True)
        o = acc1_ref[...] * w1 - acc2_ref[...] * w2
        ms = jnp.mean(o * o, axis=-1, keepdims=True)
        y = o * lax.rsqrt(ms + EPS) * dog_ref[...] * (1.0 - lam_init)
        o_ref[0, qrows, :] = y.astype(o_ref.dtype)
        return 0

    lax.fori_loop(0, seq // bq, query_block, 0)


def _attention(qkh, vh, b, lq1, lk1, lq2, lk2, dog, lam_init):
    s = qkh.shape[1] // b
    bq, bk = ATT_BQ, ATT_BK
    head_spec = lambda w, off: pl.BlockSpec((1, s, w), lambda bi, hi: (off + hi, bi, 0))
    vec_spec = lambda w: pl.BlockSpec((1, w), lambda bi, hi: (0, 0))
    row = lambda a: a.reshape(1, -1)
    f32, bf16 = jnp.float32, jnp.bfloat16
    stat = pltpu.VMEM((bq, LANES), f32)
    return pl.pallas_call(
        functools.partial(_attention_kernel, seq=s, lam_init=lam_init),
        grid=(b, N_HEADS_A),
        in_specs=[head_spec(HEAD_DIM_A, 0), head_spec(HEAD_DIM_A, N_HEADS_A),
                  head_spec(HEAD_DIM_A, 2 * N_HEADS_A), head_spec(HEAD_DIM_A, 3 * N_HEADS_A),
                  head_spec(V_DIM_A, 0),
                  vec_spec(HEAD_DIM_A), vec_spec(HEAD_DIM_A), vec_spec(HEAD_DIM_A), vec_spec(HEAD_DIM_A),
                  vec_spec(V_DIM_A)],
        out_specs=pl.BlockSpec((1, s, V_DIM_A), lambda bi, hi: (hi, bi, 0)),
        out_shape=jax.ShapeDtypeStruct((N_HEADS_A, b * s, V_DIM_A), bf16),
        scratch_shapes=[pltpu.VMEM((s, HEAD_DIM_A + LANES), bf16),
                        pltpu.VMEM((s, HEAD_DIM_A + LANES), bf16),
                        pltpu.VMEM((bq // bk, bq, bk), f32),
                        pltpu.VMEM((bq, bk), f32), pltpu.VMEM((bq, bk), f32),
                        pltpu.VMEM((bq, bk), bf16), pltpu.VMEM((bq, bk), bf16),
                        pltpu.VMEM((bq, V_DIM_A), f32), pltpu.VMEM((bq, V_DIM_A), f32),
                        stat, stat, stat, stat, stat,
                        pltpu.VMEM((bq, HEAD_DIM_A + LANES), bf16),
                        pltpu.VMEM((bq, HEAD_DIM_A + LANES), bf16)],
        compiler_params=_params(("parallel", "arbitrary")),
        name="diff_attention",
    )(qkh, qkh, qkh, qkh, vh, row(lq1), row(lk1), row(lq2), row(lk2), row(dog))


def _gate_kernel(u_ref, v_ref, g0_ref, g1_ref, ya_ref, gv_ref, ws_ref, bt_ref, o_ref, *, width):
    f32 = jnp.float32
    gd = width // N_GROUPS_B
    heads_per_group = gd // V_DIM_A

    v = jax.nn.gelu(v_ref[...].astype(f32))
    ms = jnp.mean(v * v, axis=-1, keepdims=True)
    vn = (v * lax.rsqrt(ms + EPS) * gv_ref[...]).astype(jnp.bfloat16)
    i = lax.broadcasted_iota(jnp.int32, (GMLP_CHUNK, GMLP_CHUNK), 0)
    j = lax.broadcasted_iota(jnp.int32, (GMLP_CHUNK, GMLP_CHUNK), 1)
    allowed = (j // CHUNK) <= (i // CHUNK)
    for g in range(N_GROUPS_B):
        wm = jnp.where(allowed, ws_ref[g], 0.0).astype(jnp.bfloat16)
        bias = bt_ref[:, g:g + 1]
        cols = slice(g * gd, (g + 1) * gd)
        for wdw in range(GATE_ROWS // GMLP_CHUNK):
            rows = slice(wdw * GMLP_CHUNK, (wdw + 1) * GMLP_CHUNK)
            mixed = jnp.dot(wm, vn[rows, cols], preferred_element_type=f32) + bias
            y_b = jax.nn.gelu(u_ref[rows, cols].astype(f32)) * mixed
            y_a = jnp.concatenate([ya_ref[g * heads_per_group + t, rows, :]
                                   for t in range(heads_per_group)], axis=1).astype(f32)
            merged = (jax.nn.sigmoid(g0_ref[rows, cols].astype(f32)) * y_a
                      + jax.nn.sigmoid(g1_ref[rows, cols].astype(f32)) * y_b)
            o_ref[rows, cols] = merged.astype(o_ref.dtype)


def _gate_merge(zr, yah, gv, ws, b, width):
    m = zr.shape[0]
    zspec = lambda blk: pl.BlockSpec((GATE_ROWS, width), lambda i: (i, blk))
    return pl.pallas_call(
        functools.partial(_gate_kernel, width=width),
        grid=(m // GATE_ROWS,),
        in_specs=[zspec(0), zspec(1), zspec(2), zspec(3),
                  pl.BlockSpec((N_HEADS_A, GATE_ROWS, V_DIM_A), lambda i: (0, i, 0)),
                  pl.BlockSpec((1, width), lambda i: (0, 0)),
                  pl.BlockSpec((N_GROUPS_B, GMLP_CHUNK, GMLP_CHUNK), lambda i: (0, 0, 0)),
                  pl.BlockSpec((GMLP_CHUNK, N_GROUPS_B), lambda i: (0, 0))],
        out_specs=pl.BlockSpec((GATE_ROWS, width), lambda i: (i, 0)),
        out_shape=jax.ShapeDtypeStruct((m, width), jnp.bfloat16),
        compiler_params=_params(("parallel",)),
        name="gmlp_gate_merge",
    )(zr, zr, zr, zr, yah, gv.reshape(1, width), ws, b.T)


def _lambda_init(layer_idx):
    return 0.8 - 0.6 * math.exp(-0.3 * (layer_idx - 1))


def kernel(x, norm_mix_g, w_in, qk_gain_q, qk_gain_k, lam_q1, lam_k1, lam_q2, lam_k2, diff_out_g,
           gmlp_v_g, gmlp_ws, gmlp_b, w_o, norm_mlp_g, w_up, w_down):
    b, s, d = x.shape
    depth = w_in.shape[0]
    qk_cols = N_HEADS_A * HEAD_DIM_A
    qkv_end = 4 * qk_cols
    v_cols = N_HEADS_A * V_DIM_A
    qscale = HEAD_DIM_A ** -0.5 * LOG2E
    xf = x.reshape(b * s, d)
    for l in range(depth):
        h = _rmsnorm(xf, norm_mix_g[l])
        gq = jnp.tile(qk_gain_q[l] * qscale, 2 * N_HEADS_A)
        gk = jnp.tile(qk_gain_k[l], 2 * N_HEADS_A)
        qkh = _wcast_matmul(h, w_in, l, col_off=0, ncols=qkv_end, epilogue="headnorm",
                            colgain=jnp.concatenate([gq, gk]), head_width=HEAD_DIM_A)
        vh = _wcast_matmul(h, w_in, l, col_off=qkv_end, ncols=v_cols, epilogue="heads",
                           head_width=V_DIM_A)
        zr = _wcast_matmul(h, w_in, l, col_off=qkv_end + v_cols)
        yah = _attention(qkh, vh, b, lam_q1[l], lam_k1[l], lam_q2[l], lam_k2[l], diff_out_g[l],
                         _lambda_init(l + 1))
        merged = _gate_merge(zr, yah, gmlp_v_g[l], gmlp_ws[l], gmlp_b[l], d)
        xf = _wcast_matmul(merged, w_o, l, bn=MM_BN // 2, epilogue="residual", residual=xf,
                           out_dtype=jnp.float32)
        h2 = _rmsnorm(xf, norm_mlp_g[l])
        a, w_down_bf16 = _wcast_matmul(h2, w_up, l, epilogue="relu2", side=w_down)
        xf = _ksplit_matmul(a, w_down_bf16, xf)
    return xf.reshape(b, s, d)
```

```python
import functools
import math

import jax
import jax.numpy as jnp
from jax import lax
from jax.experimental import pallas as pl
from jax.experimental.pallas import tpu as pltpu

CHUNK = 64
N_HEADS_A = 16
HEAD_DIM_A = 128
V_DIM_A = 2 * HEAD_DIM_A
GMLP_CHUNK = 128
N_GROUPS_B = 8
EPS = 1e-6
MASK_VALUE = -1e30
LOG2E = math.log2(math.e)

V7X_VMEM_BYTES = 64 * 1024 * 1024
VMEM_LIMIT_BYTES = V7X_VMEM_BYTES - 8 * 1024 * 1024
LANES = 128

MM_BM = 1024
MM_BN = 1024
MM_BK = 4096
NORM_ROWS = 256
ATT_BQ = 1024
ATT_BK = 512
ATT_STRIP = 64
GATE_ROWS = 256


def _params(semantics):
    return pltpu.CompilerParams(dimension_semantics=semantics, vmem_limit_bytes=VMEM_LIMIT_BYTES)


def _rmsnorm_kernel(x_ref, g_ref, o_ref):
    x = x_ref[...].astype(jnp.float32)
    ms = jnp.mean(x * x, axis=-1, keepdims=True)
    o_ref[...] = (x * lax.rsqrt(ms + EPS) * g_ref[...]).astype(o_ref.dtype)


def _rmsnorm(x, g):
    m, d = x.shape
    return pl.pallas_call(
        _rmsnorm_kernel,
        grid=(m // NORM_ROWS,),
        in_specs=[pl.BlockSpec((NORM_ROWS, d), lambda i: (i, 0)),
                  pl.BlockSpec((1, d), lambda i: (0, 0))],
        out_specs=pl.BlockSpec((NORM_ROWS, d), lambda i: (i, 0)),
        out_shape=jax.ShapeDtypeStruct((m, d), jnp.bfloat16),
        compiler_params=_params(("parallel",)),
        name="rmsnorm",
    )(x, g.reshape(1, d))


def _wcast_matmul_kernel(*refs, epilogue, has_side, chunk, head_width, bf16_copy):
    refs = list(refs)
    x_ref, w_ref = refs[0], refs[1]
    pos = 2
    r_ref = g_ref = side_in = side_out = ob_ref = None
    if epilogue == "residual":
        r_ref = refs[pos]; pos += 1
    if epilogue == "headnorm":
        g_ref = refs[pos]; pos += 1
    if has_side:
        side_in = refs[pos]; pos += 1
    o_ref = refs[pos]; pos += 1
    if bf16_copy:
        ob_ref = refs[pos]; pos += 1
    if has_side:
        side_out = refs[pos]; pos += 1
    panel0, panel1 = refs[pos], refs[pos + 1]

    n = pl.program_id(0)
    m = pl.program_id(1)
    rows = pl.ds(pl.multiple_of(m * chunk, chunk), chunk)

    def cast_chunk(panel):
        panel[rows, :] = w_ref[...].astype(panel.dtype)

    def compute(panel):
        acc = jnp.dot(x_ref[...], panel[...], preferred_element_type=jnp.float32)
        if epilogue in ("headnorm", "heads"):
            for t in range(acc.shape[1] // head_width):
                cols = slice(t * head_width, (t + 1) * head_width)
                xh = acc[:, cols]
                if epilogue == "headnorm":
                    ms = jnp.mean(xh * xh, axis=-1, keepdims=True)
                    xh = xh * lax.rsqrt(ms + EPS) * g_ref[:, cols]
                o_ref[t] = xh.astype(o_ref.dtype)
        else:
            if epilogue == "relu2":
                acc = jnp.square(jnp.maximum(acc, 0.0))
            elif epilogue == "residual":
                acc = r_ref[...] + acc
            o_ref[...] = acc.astype(o_ref.dtype)
            if bf16_copy:
                ob_ref[...] = acc.astype(ob_ref.dtype)
        if has_side:
            side_out[...] = side_in[...].astype(side_out.dtype)

    @pl.when(n == 0)
    def _():
        cast_chunk(panel0)

    @pl.when(jnp.logical_and(n > 0, n % 2 == 1))
    def _():
        cast_chunk(panel1)
        compute(panel0)

    @pl.when(jnp.logical_and(n > 0, n % 2 == 0))
    def _():
        cast_chunk(panel0)
        compute(panel1)


def _wcast_matmul(x, w, layer, *, col_off=0, ncols=None, bn=MM_BN, epilogue="none", residual=None,
                  colgain=None, head_width=None, out_dtype=jnp.bfloat16, side=None,
                  bf16_copy=False):
    m, kdim = x.shape
    n = w.shape[2] - col_off if ncols is None else ncols
    bm = min(MM_BM, m)
    nm, nn = m // bm, n // bn
    chunk = kdim // nm
    col0 = col_off // bn
    row_blk = lambda ni, mi: jnp.where(ni > 0, mi, 0)
    col_blk = lambda ni: jnp.maximum(ni - 1, 0)
    in_specs = [pl.BlockSpec((bm, kdim), lambda ni, mi: (row_blk(ni, mi), 0)),
                pl.BlockSpec((None, chunk, bn),
                             lambda ni, mi: (layer, jnp.where(ni < nn, mi, nm - 1),
                                             col0 + jnp.minimum(ni, nn - 1)))]
    args = [x, w]
    if epilogue == "residual":
        in_specs.append(pl.BlockSpec((bm, bn), lambda ni, mi: (row_blk(ni, mi), col_blk(ni))))
        args.append(residual)
    if epilogue == "headnorm":
        in_specs.append(pl.BlockSpec((1, bn), lambda ni, mi: (0, col_blk(ni))))
        args.append(colgain.reshape(1, n))
    if head_width is None:
        out_specs = [pl.BlockSpec((bm, bn), lambda ni, mi: (row_blk(ni, mi), col_blk(ni)))]
        out_shape = [jax.ShapeDtypeStruct((m, n), out_dtype)]
        if bf16_copy:
            out_specs.append(out_specs[0])
            out_shape.append(jax.ShapeDtypeStruct((m, n), jnp.bfloat16))
    else:
        out_specs = [pl.BlockSpec((bn // head_width, bm, head_width),
                                  lambda ni, mi: (col_blk(ni), row_blk(ni, mi), 0))]
        out_shape = [jax.ShapeDtypeStruct((n // head_width, m, head_width), out_dtype)]
    if side is not None:
        _, srows, scols = side.shape
        sblk = srows // (nn * nm)
        side_idx = lambda ni, mi: (col_blk(ni) * nm + row_blk(ni, mi), 0)
        in_specs.append(pl.BlockSpec((None, sblk, scols), lambda ni, mi: (layer,) + side_idx(ni, mi)))
        args.append(side)
        out_specs.append(pl.BlockSpec((sblk, scols), side_idx))
        out_shape.append(jax.ShapeDtypeStruct((srows, scols), jnp.bfloat16))
    outs = pl.pallas_call(
        functools.partial(_wcast_matmul_kernel, epilogue=epilogue, has_side=side is not None,
                          chunk=chunk, head_width=head_width, bf16_copy=bf16_copy),
        grid=(nn + 1, nm),
        in_specs=in_specs,
        out_specs=out_specs,
        out_shape=out_shape,
        scratch_shapes=[pltpu.VMEM((kdim, bn), jnp.bfloat16), pltpu.VMEM((kdim, bn), jnp.bfloat16)],
        compiler_params=_params(("arbitrary", "arbitrary")),
        name="wcast_matmul_" + epilogue,
    )(*args)
    return outs if (side is not None or bf16_copy) else outs[0]


def _ksplit_matmul_kernel(x_ref, w_ref, r_ref, o_ref):
    k = pl.program_id(2)

    @pl.when(k == 0)
    def _():
        o_ref[...] = r_ref[...] + jnp.dot(x_ref[...], w_ref[...], preferred_element_type=jnp.float32)

    @pl.when(k > 0)
    def _():
        o_ref[...] += jnp.dot(x_ref[...], w_ref[...], preferred_element_type=jnp.float32)


def _ksplit_matmul(x, w, residual):
    m, kdim = x.shape
    _, n = w.shape
    bm, bn, bk = min(MM_BM, m), min(MM_BN, n), min(MM_BK, kdim)
    return pl.pallas_call(
        _ksplit_matmul_kernel,
        grid=(m // bm, n // bn, kdim // bk),
        in_specs=[pl.BlockSpec((bm, bk), lambda i, j, k: (i, k)),
                  pl.BlockSpec((bk, bn), lambda i, j, k: (k, j)),
                  pl.BlockSpec((bm, bn), lambda i, j, k: (i, j))],
        out_specs=pl.BlockSpec((bm, bn), lambda i, j, k: (i, j)),
        out_shape=jax.ShapeDtypeStruct((m, n), jnp.float32),
        compiler_params=_params(("parallel", "parallel", "arbitrary")),
        name="ksplit_matmul_residual",
    )(x, w, residual)


def _attention_kernel(q1_ref, q2_ref, k1_ref, k2_ref, v_ref,
                      lq1_ref, lk1_ref, lq2_ref, lk2_ref, dog_ref, o_ref,
                      kn1_ref, kn2_ref, dbias_ref, s1_ref, s2_ref, p1_ref, p2_ref,
                      acc1_ref, acc2_ref, m1_ref, l1_ref, a1_ref, m2_ref, l2_ref, q1a_ref, q2a_ref,
                      *, seq, lam_init):
    h = pl.program_id(1)
    bq, bk = ATT_BQ, ATT_BK
    f32, bf16 = jnp.float32, jnp.bfloat16

    slope = jnp.exp2(jnp.full((1, 1), -8.0 / N_HEADS_A, f32) * (h + 1).astype(f32)) * LOG2E
    lane = lax.broadcasted_iota(jnp.int32, (bk, LANES), 1)

    def key_chunk(i, _):
        rows = pl.ds(pl.multiple_of(i * bk, bk), bk)
        pos = (lax.broadcasted_iota(jnp.int32, (bk, 1), 0) + i * bk).astype(f32)
        bias = slope * pos
        hi = bias.astype(bf16).astype(f32)
        mid = (bias - hi).astype(bf16).astype(f32)
        lo = (bias - hi - mid).astype(bf16).astype(f32)
        extra = jnp.where(lane == 0, hi, jnp.where(lane == 1, mid, jnp.where(lane == 2, lo, 0.0)))
        extra = extra.astype(bf16)
        kn1_ref[rows, :HEAD_DIM_A] = k1_ref[0, rows, :]
        kn2_ref[rows, :HEAD_DIM_A] = k2_ref[0, rows, :]
        kn1_ref[rows, HEAD_DIM_A:] = extra
        kn2_ref[rows, HEAD_DIM_A:] = extra
        return 0

    lax.fori_loop(0, seq // bk, key_chunk, 0)
    qlane = lax.broadcasted_iota(jnp.int32, (bq, LANES), 1)
    ones = jnp.where(qlane < 3, 1.0, 0.0).astype(bf16)
    q1a_ref[:, HEAD_DIM_A:] = ones
    q2a_ref[:, HEAD_DIM_A:] = ones
    n_diag = bq // bk
    for d in range(n_diag):
        r = lax.broadcasted_iota(jnp.int32, (bq, bk), 0)
        c = lax.broadcasted_iota(jnp.int32, (bq, bk), 1) + d * bk
        future = jnp.maximum(c - r, 0).astype(f32)
        dbias_ref[d] = jnp.where((c // CHUNK) <= (r // CHUNK), -2.0 * slope * future, MASK_VALUE)

    lam = (jnp.exp(jnp.sum(lq1_ref[...] * lk1_ref[...], keepdims=True))
           - jnp.exp(jnp.sum(lq2_ref[...] * lk2_ref[...], keepdims=True)) + lam_init)

    def scores(q, k):
        return lax.dot_general(q, k, (((1,), (1,)), ((), ())), preferred_element_type=f32)

    def softmax(s_ref, p_ref, m_ref, l_ref, diag, row0):
        live = slice(row0, bq)
        m_prev = m_ref[live, :]
        s_all = s_ref[live, :] if diag is None else s_ref[live, :] + dbias_ref[diag, live, :]
        m_next = jnp.maximum(m_prev, jnp.max(s_all, axis=1, keepdims=True))
        alpha = jnp.exp2(m_prev - m_next)
        m_ref[live, :] = m_next
        for t in range((bq - row0) // ATT_STRIP):
            part = slice(t * ATT_STRIP, (t + 1) * ATT_STRIP)
            rows = slice(row0 + t * ATT_STRIP, row0 + (t + 1) * ATT_STRIP)
            s = s_ref[rows, :] if diag is None else s_ref[rows, :] + dbias_ref[diag, rows, :]
            p = jnp.exp2(s - jnp.tile(m_next[part, :], (1, bk // LANES)))
            psum = p[:, :LANES]
            for j in range(1, bk // LANES):
                psum = psum + p[:, j * LANES:(j + 1) * LANES]
            l_ref[rows, :] = alpha[part, :] * l_ref[rows, :] + psum
            p_ref[rows, :] = p.astype(p_ref.dtype)
        return alpha

    def accumulate(acc_ref, alpha, p_ref, v, row0):
        live = slice(row0, bq)
        acc_ref[live, :] = (jnp.tile(alpha, (1, V_DIM_A // LANES)) * acc_ref[live, :]
                            + jnp.dot(p_ref[live, :], v, preferred_element_type=f32))

    def key_rows(kj):
        return pl.ds(pl.multiple_of(kj * bk, bk), bk)

    def reset_state():
        for ref in (acc1_ref, acc2_ref, l1_ref, l2_ref, p1_ref):
            ref[...] = jnp.zeros_like(ref)
        m1_ref[...] = jnp.full_like(m1_ref, MASK_VALUE)
        m2_ref[...] = jnp.full_like(m2_ref, MASK_VALUE)
        a1_ref[...] = jnp.ones_like(a1_ref)

    def query_block(qi, _):
        qrows = pl.ds(pl.multiple_of(qi * bq, bq), bq)
        q1a_ref[:, :HEAD_DIM_A] = q1_ref[0, qrows, :]
        q2a_ref[:, :HEAD_DIM_A] = q2_ref[0, qrows, :]

        s2_ref[...] = scores(q2a_ref[...], kn2_ref[key_rows(0), :])
        reset_state()

        def stage(kb, diag, prev_row0, row0, next_row0):
            accumulate(acc1_ref, a1_ref[prev_row0:, :], p1_ref,
                       v_ref[0, key_rows(jnp.maximum(kb - 1, 0)), :], prev_row0)
            alpha2 = softmax(s2_ref, p2_ref, m2_ref, l2_ref, diag, row0)
            s1_ref[row0:, :] = scores(q1a_ref[row0:, :], kn1_ref[key_rows(kb), :])
            a1_ref[row0:, :] = softmax(s1_ref, p1_ref, m1_ref, l1_ref, diag, row0)
            accumulate(acc2_ref, alpha2, p2_ref, v_ref[0, key_rows(kb), :], row0)
            if next_row0 is not None:
                s2_ref[next_row0:, :] = scores(q2a_ref[next_row0:, :], kn2_ref[key_rows(kb + 1), :])

        def past_block(kj, _):
            stage(kj, None, 0, 0, 0)
            return 0

        first_diag = qi * n_diag
        lax.fori_loop(0, first_diag, past_block, 0)
        for d in range(n_diag):
            stage(first_diag + d, d, max(d - 1, 0) * bk, d * bk,
                  (d + 1) * bk if d + 1 < n_diag else None)
        last_row0 = (n_diag - 1) * bk
        accumulate(acc1_ref, a1_ref[last_row0:, :], p1_ref,
                   v_ref[0, key_rows(first_diag + n_diag - 1), :], last_row0)

        w1 = 1.0 / jnp.sum(l1_ref[...], axis=1, keepdims=True)
        w2 = lam / jnp.sum(l2_ref[...], axis=1, keepdims=True)
        o = acc1_ref[...] * w1 - acc2_ref[...] * w2
        ms = jnp.mean(o * o, axis=-1, keepdims=True)
        y = o * lax.rsqrt(ms + EPS) * dog_ref[...] * (1.0 - lam_init)
        o_ref[0, qrows, :] = y.astype(o_ref.dtype)
        return 0

    lax.fori_loop(0, seq // bq, query_block, 0)


def _attention(qkh, vh, b, lq1, lk1, lq2, lk2, dog, lam_init):
    s = qkh.shape[1] // b
    bq, bk = ATT_BQ, ATT_BK
    head_spec = lambda w, off: pl.BlockSpec((1, s, w), lambda bi, hi: (off + hi, bi, 0))
    vec_spec = lambda w: pl.BlockSpec((1, w), lambda bi, hi: (0, 0))
    row = lambda a: a.reshape(1, -1)
    f32, bf16 = jnp.float32, jnp.bfloat16
    stat = pltpu.VMEM((bq, LANES), f32)
    return pl.pallas_call(
        functools.partial(_attention_kernel, seq=s, lam_init=lam_init),
        grid=(b, N_HEADS_A),
        in_specs=[head_spec(HEAD_DIM_A, 0), head_spec(HEAD_DIM_A, N_HEADS_A),
                  head_spec(HEAD_DIM_A, 2 * N_HEADS_A), head_spec(HEAD_DIM_A, 3 * N_HEADS_A),
                  head_spec(V_DIM_A, 0),
                  vec_spec(HEAD_DIM_A), vec_spec(HEAD_DIM_A), vec_spec(HEAD_DIM_A), vec_spec(HEAD_DIM_A),
                  vec_spec(V_DIM_A)],
        out_specs=pl.BlockSpec((1, s, V_DIM_A), lambda bi, hi: (hi, bi, 0)),
        out_shape=jax.ShapeDtypeStruct((N_HEADS_A, b * s, V_DIM_A), bf16),
        scratch_shapes=[pltpu.VMEM((s, HEAD_DIM_A + LANES), bf16),
                        pltpu.VMEM((s, HEAD_DIM_A + LANES), bf16),
                        pltpu.VMEM((bq // bk, bq, bk), f32),
                        pltpu.VMEM((bq, bk), f32), pltpu.VMEM((bq, bk), f32),
                        pltpu.VMEM((bq, bk), bf16), pltpu.VMEM((bq, bk), bf16),
                        pltpu.VMEM((bq, V_DIM_A), f32), pltpu.VMEM((bq, V_DIM_A), f32),
                        stat, stat, stat, stat, stat,
                        pltpu.VMEM((bq, HEAD_DIM_A + LANES), bf16),
                        pltpu.VMEM((bq, HEAD_DIM_A + LANES), bf16)],
        compiler_params=_params(("parallel", "arbitrary")),
        name="diff_attention",
    )(qkh, qkh, qkh, qkh, vh, row(lq1), row(lk1), row(lq2), row(lk2), row(dog))


def _gate_kernel(u_ref, v_ref, g0_ref, g1_ref, ya_ref, gv_ref, ws_ref, bt_ref, o_ref, *, width):
    f32 = jnp.float32
    gd = width // N_GROUPS_B
    heads_per_group = gd // V_DIM_A

    v = jax.nn.gelu(v_ref[...].astype(f32))
    ms = jnp.mean(v * v, axis=-1, keepdims=True)
    vn = (v * lax.rsqrt(ms + EPS) * gv_ref[...]).astype(jnp.bfloat16)
    i = lax.broadcasted_iota(jnp.int32, (GMLP_CHUNK, GMLP_CHUNK), 0)
    j = lax.broadcasted_iota(jnp.int32, (GMLP_CHUNK, GMLP_CHUNK), 1)
    allowed = (j // CHUNK) <= (i // CHUNK)
    for g in range(N_GROUPS_B):
        wm = jnp.where(allowed, ws_ref[g], 0.0).astype(jnp.bfloat16)
        bias = bt_ref[:, g:g + 1]
        cols = slice(g * gd, (g + 1) * gd)
        for wdw in range(GATE_ROWS // GMLP_CHUNK):
            rows = slice(wdw * GMLP_CHUNK, (wdw + 1) * GMLP_CHUNK)
            mixed = jnp.dot(wm, vn[rows, cols], preferred_element_type=f32) + bias
            y_b = jax.nn.gelu(u_ref[rows, cols].astype(f32)) * mixed
            y_a = jnp.concatenate([ya_ref[g * heads_per_group + t, rows, :]
                                   for t in range(heads_per_group)], axis=1).astype(f32)
            merged = (jax.nn.sigmoid(g0_ref[rows, cols].astype(f32)) * y_a
                      + jax.nn.sigmoid(g1_ref[rows, cols].astype(f32)) * y_b)
            o_ref[rows, cols] = merged.astype(o_ref.dtype)


def _gate_merge(zr, yah, gv, ws, b, width):
    m = zr.shape[0]
    zspec = lambda blk: pl.BlockSpec((GATE_ROWS, width), lambda i: (i, blk))
    return pl.pallas_call(
        functools.partial(_gate_kernel, width=width),
        grid=(m // GATE_ROWS,),
        in_specs=[zspec(0), zspec(1), zspec(2), zspec(3),
                  pl.BlockSpec((N_HEADS_A, GATE_ROWS, V_DIM_A), lambda i: (0, i, 0)),
                  pl.BlockSpec((1, width), lambda i: (0, 0)),
                  pl.BlockSpec((N_GROUPS_B, GMLP_CHUNK, GMLP_CHUNK), lambda i: (0, 0, 0)),
                  pl.BlockSpec((GMLP_CHUNK, N_GROUPS_B), lambda i: (0, 0))],
        out_specs=pl.BlockSpec((GATE_ROWS, width), lambda i: (i, 0)),
        out_shape=jax.ShapeDtypeStruct((m, width), jnp.bfloat16),
        compiler_params=_params(("parallel",)),
        name="gmlp_gate_merge",
    )(zr, zr, zr, zr, yah, gv.reshape(1, width), ws, b.T)


def _lambda_init(layer_idx):
    return 0.8 - 0.6 * math.exp(-0.3 * (layer_idx - 1))


def kernel(x, norm_mix_g, w_in, qk_gain_q, qk_gain_k, lam_q1, lam_k1, lam_q2, lam_k2, diff_out_g,
           gmlp_v_g, gmlp_ws, gmlp_b, w_o, norm_mlp_g, w_up, w_down):
    b, s, d = x.shape
    depth = w_in.shape[0]
    qk_cols = N_HEADS_A * HEAD_DIM_A
    qkv_end = 4 * qk_cols
    v_cols = N_HEADS_A * V_DIM_A
    qscale = HEAD_DIM_A ** -0.5 * LOG2E
    xf = x.reshape(b * s, d)
    for l in range(depth):
        h = _rmsnorm(xf, norm_mix_g[l])
        gq = jnp.tile(qk_gain_q[l] * qscale, 2 * N_HEADS_A)
        gk = jnp.tile(qk_gain_k[l], 2 * N_HEADS_A)
        qkh = _wcast_matmul(h, w_in, l, col_off=0, ncols=qkv_end, epilogue="headnorm",
                            colgain=jnp.concatenate([gq, gk]), head_width=HEAD_DIM_A)
        vh = _wcast_matmul(h, w_in, l, col_off=qkv_end, ncols=v_cols, epilogue="heads",
                           head_width=V_DIM_A)
        zr = _wcast_matmul(h, w_in, l, col_off=qkv_end + v_cols)
        yah = _attention(qkh, vh, b, lam_q1[l], lam_k1[l], lam_q2[l], lam_k2[l], diff_out_g[l],
                         _lambda_init(l + 1))
        merged = _gate_merge(zr, yah, gmlp_v_g[l], gmlp_ws[l], gmlp_b[l], d)
        xf, xb = _wcast_matmul(merged, w_o, l, bn=MM_BN // 2, epilogue="residual", residual=xf,
                               out_dtype=jnp.float32, bf16_copy=True)
        h2 = _rmsnorm(xb, norm_mlp_g[l])
        a, w_down_bf16 = _wcast_matmul(h2, w_up, l, epilogue="relu2", side=w_down)
        xf = _ksplit_matmul(a, w_down_bf16, xf)
    return xf.reshape(b, s, d)
```
